```python
import math
import jax, jax.numpy as jnp
from jax import lax
import numpy as np

D_MODEL = 4096
BATCH = 2
SEQ = 4096
DEPTH = 4

CHUNK = 64
PLE_DIM = 256

D_MIX = D_MODEL
D_DELTA = D_MIX // 2
D_RWKV = D_MIX - D_DELTA
GDN_HEAD_DIM = 128
GDN_HEADS = D_DELTA // GDN_HEAD_DIM
GDN_CONV = 4
RWKV_HEAD_DIM = 64
RWKV_HEADS = D_RWKV // RWKV_HEAD_DIM
DECAY_LORA = 96
AAA_LORA = 96
GATE_LORA = 64
RWKV_LNX_EPS = 64e-5

GDN_COLS = 4 * D_DELTA + 2 * GDN_HEADS
RWKV_COLS = 3 * D_RWKV + DECAY_LORA + AAA_LORA + GATE_LORA
IN_COLS = GDN_COLS + RWKV_COLS

N_GROUPS = 4
EXPERTS_PER_GROUP = 8
N_EXPERTS = N_GROUPS * EXPERTS_PER_GROUP
TOP_K_IN_GROUP = 2
D_EXPERT = 256

DN_ALPHA = (2 * DEPTH) ** 0.25
DN_BETA = (8 * DEPTH) ** -0.25

kernel_name = 'hybrid_gdn_rwkv7_hier_moe_deepnorm'


def _split(h, sizes):
    parts, start = [], 0
    for s in sizes:
        parts.append(h[..., start:start + s])
        start += s
    return parts


def _l2norm(t, eps=1e-6):
    return t * lax.rsqrt(jnp.sum(t * t, axis=-1, keepdims=True) + eps)


def _layer_norm(x, g, b, eps=1e-5):
    xf = x.astype(jnp.float32)
    xc = xf - jnp.mean(xf, axis=-1, keepdims=True)
    var = jnp.mean(xc * xc, axis=-1, keepdims=True)
    y = xc * lax.rsqrt(var + eps) * g.astype(jnp.float32) + b.astype(jnp.float32)
    return y.astype(x.dtype)


def _causal_depthwise_conv(x, w):
    K, T = w.shape[0], x.shape[1]
    xp = jnp.pad(x, ((0, 0), (K - 1, 0), (0, 0)))
    y = xp[:, 0:T] * w[0]
    for j in range(1, K):
        y = y + xp[:, j:j + T] * w[j]
    return y


def _chunk_gated_delta_rule(q, k, v, g, beta):
    B, T, H, K = q.shape
    V = v.shape[-1]
    N = T // CHUNK

    def blocks(t):
        t = t.reshape((B, N, CHUNK, H) + t.shape[3:])
        return jnp.moveaxis(t, 3, 1)

    q, k, v, g, beta = blocks(q), blocks(k), blocks(v), blocks(g), blocks(beta)
    g = jnp.cumsum(g, axis=-1)
    idx = jnp.arange(CHUNK)
    causal = idx[:, None] >= idx[None, :]
    strict = idx[:, None] > idx[None, :]
    decay = jnp.exp(jnp.where(causal, g[..., :, None] - g[..., None, :], -jnp.inf))
    k_beta = k * beta[..., None]
    a_low = jnp.where(strict, jnp.einsum('bhnik,bhnjk->bhnij', k_beta, k) * decay, 0.0)
    tmat = a_low + jnp.eye(CHUNK, dtype=a_low.dtype)
    rhs = jnp.concatenate([v * beta[..., None], k_beta * jnp.exp(g)[..., None]], axis=-1)
    sol = lax.linalg.triangular_solve(tmat, rhs, left_side=True, lower=True, unit_diagonal=True)
    u, w = sol[..., :V], sol[..., V:]
    attn = jnp.einsum('bhnik,bhnjk->bhnij', q, k) * decay
    q_dec = q * jnp.exp(g)[..., None]
    k_dec = k * jnp.exp(g[..., -1:] - g)[..., None]
    g_last = jnp.exp(g[..., -1])

    def step(S, inp):
        q_i, k_i, u_i, w_i, a_i, gl_i = inp
        v_new = u_i - jnp.einsum('bhck,bhkv->bhcv', w_i, S)
        o = jnp.einsum('bhck,bhkv->bhcv', q_i, S) + jnp.einsum('bhij,bhjv->bhiv', a_i, v_new)
        S = S * gl_i[..., None, None] + jnp.einsum('bhck,bhcv->bhkv', k_i, v_new)
        return S, o

    xs = tuple(jnp.moveaxis(t, 2, 0) for t in (q_dec, k_dec, u, w, attn, g_last))
    S0 = jnp.zeros((B, H, K, V), q.dtype)
    _, o = lax.scan(step, S0, xs)
    o = jnp.moveaxis(o, 0, 2).reshape(B, H, T, V)
    return jnp.transpose(o, (0, 2, 1, 3))


def _gated_deltanet_group(q, k, v, z, a, b, conv_w, a_log, dt_bias, norm_w):
    B, T, _ = q.shape
    f32 = jnp.float32
    qkv = jax.nn.silu(_causal_depthwise_conv(jnp.concatenate([q, k, v], axis=-1), conv_w))
    q, k, v = (t.astype(f32).reshape(B, T, GDN_HEADS, GDN_HEAD_DIM) for t in _split(qkv, [D_DELTA] * 3))
    q = _l2norm(q) * (GDN_HEAD_DIM ** -0.5)
    k = _l2norm(k)
    g = -jnp.exp(a_log.astype(f32)) * jax.nn.softplus(a.astype(f32) + dt_bias.astype(f32))
    beta = jax.nn.sigmoid(b.astype(f32))
    o = _chunk_gated_delta_rule(q, k, v, g, beta)
    o = o * lax.rsqrt(jnp.mean(o * o, axis=-1, keepdims=True) + 1e-6) * norm_w.astype(f32)
    o = o.reshape(B, T, D_DELTA) * jax.nn.silu(z.astype(f32))
    return o.astype(z.dtype)


def _rwkv7_recurrence(r, w, k, v, a, b):
    B, T, H, N = r.shape

    def step(S, inp):
        r_t, w_t, k_t, v_t, a_t, b_t = inp
        sa = jnp.einsum('bhij,bhj->bhi', S, a_t)
        S = S * w_t[:, :, None, :] + sa[..., None] * b_t[:, :, None, :] + v_t[..., None] * k_t[:, :, None, :]
        return S, jnp.einsum('bhij,bhj->bhi', S, r_t)

    xs = tuple(jnp.moveaxis(t, 1, 0) for t in (r, w, k, v, a, b))
    S0 = jnp.zeros((B, H, N, N), r.dtype)
    _, y = lax.scan(step, S0, xs)
    return jnp.moveaxis(y, 0, 1)


def _rwkv7_group(h, mu, w0, w2, a0, a2, g2, k_k, k_a, r_k, lnx_g, lnx_b):
    B, T, _ = h.shape
    f32 = jnp.float32
    h_prev = jnp.pad(h, ((0, 0), (1, 0), (0, 0)))[:, :-1]
    h = h + mu * (h_prev - h)
    r, k, v, xw, xa, xg = _split(h, [D_RWKV] * 3 + [DECAY_LORA, AAA_LORA, GATE_LORA])
    w_log = -jax.nn.softplus(-(w0 + jnp.tanh(xw) @ w2).astype(f32)) - 0.5
    decay = jnp.exp(-jnp.exp(w_log))
    a = jax.nn.sigmoid((a0 + xa @ a2).astype(f32))
    g = jax.nn.sigmoid(xg) @ g2
    heads = lambda t: t.astype(f32).reshape(t.shape[:-1] + (RWKV_HEADS, RWKV_HEAD_DIM))
    rf, kf, vf, af, wf = heads(r), heads(k), heads(v), heads(a), heads(decay)
    kk = _l2norm(kf * heads(k_k))
    kf = kf * (1.0 + (af - 1.0) * heads(k_a))
    y = _rwkv7_recurrence(rf, wf, kf, vf, -kk, kk * af)
    yc = y - jnp.mean(y, axis=-1, keepdims=True)
    y = yc * lax.rsqrt(jnp.mean(yc * yc, axis=-1, keepdims=True) + RWKV_LNX_EPS)
    y = y * heads(lnx_g) + heads(lnx_b)
    y = y + jnp.sum(rf * kf * r_k.astype(f32), axis=-1, keepdims=True) * vf
    return (y.reshape(B, T, D_RWKV) * g.astype(f32)).astype(h.dtype)


def _hierarchical_moe(x, w_grp, b_grp, w_rt, b_rt, w_gate, w_up, w_down):
    B, T, _ = x.shape
    f32 = jnp.float32
    xf = x.astype(f32)
    grp_prob = jax.nn.softmax(xf @ w_grp.astype(f32) + b_grp.astype(f32), axis=-1)
    p_grp, g_sel = lax.top_k(grp_prob, 1)
    ex_logits = (xf @ w_rt.astype(f32) + b_rt.astype(f32)).reshape(B, T, N_GROUPS, EXPERTS_PER_GROUP)
    ex_in_grp = jnp.take_along_axis(ex_logits, g_sel[..., None], axis=2)[:, :, 0]
    top_v, top_i = lax.top_k(ex_in_grp, TOP_K_IN_GROUP)
    gate = jax.nn.softmax(top_v, axis=-1) * p_grp
    e_idx = g_sel * EXPERTS_PER_GROUP + top_i
    combine = jnp.sum(jax.nn.one_hot(e_idx, N_EXPERTS, dtype=f32) * gate[..., None], axis=-2)
    hg = jnp.einsum('btd,edf->btef', x, w_gate)
    hu = jnp.einsum('btd,edf->btef', x, w_up)
    hact = jax.nn.silu(hg) * hu * combine.astype(x.dtype)[..., None]
    return jnp.einsum('btef,efd->btd', hact, w_down)


def setup_inputs(seed: int = 0) -> dict:
    key = jax.random.key(seed)
    ks = jax.random.split(key, 40)
    f32 = jnp.float32
    L = DEPTH
    nrm = lambda kk, shape, s: jax.random.normal(kk, shape, f32) * s
    dt = jnp.exp(jax.random.uniform(ks[5], (L, GDN_HEADS), f32, math.log(1e-3), math.log(1e-1)))
    return {
        'x': nrm(ks[0], (BATCH, SEQ, D_MODEL), 1.0),
        'p': nrm(ks[1], (DEPTH, BATCH, SEQ, PLE_DIM), 1.0),
        'w_in': nrm(ks[2], (L, D_MODEL, IN_COLS), D_MODEL ** -0.5),
        'gdn_conv_w': nrm(ks[3], (L, GDN_CONV, 3 * D_DELTA), GDN_CONV ** -0.5),
        'gdn_a_log': jnp.log(jax.random.uniform(ks[4], (L, GDN_HEADS), f32, 1.0, 16.0)),
        'gdn_dt_bias': dt + jnp.log(-jnp.expm1(-dt)),
        'gdn_norm_w': 1.0 + nrm(ks[6], (L, GDN_HEAD_DIM), 0.02),
        'rwkv_mu': jax.random.uniform(ks[7], (L, RWKV_COLS), f32),
        'rwkv_w0': jax.random.uniform(ks[8], (L, D_RWKV), f32, -5.0, -1.0),
        'rwkv_w2': nrm(ks[9], (L, DECAY_LORA, D_RWKV), 0.5 * DECAY_LORA ** -0.5),
        'rwkv_a0': nrm(ks[10], (L, D_RWKV), 0.1),
        'rwkv_a2': nrm(ks[11], (L, AAA_LORA, D_RWKV), AAA_LORA ** -0.5),
        'rwkv_g2': nrm(ks[12], (L, GATE_LORA, D_RWKV), GATE_LORA ** -0.5),
        'rwkv_k_k': 0.85 + nrm(ks[13], (L, D_RWKV), 0.02),
        'rwkv_k_a': 1.0 + nrm(ks[14], (L, D_RWKV), 0.02),
        'rwkv_r_k': nrm(ks[15], (L, RWKV_HEADS, RWKV_HEAD_DIM), 0.1),
        'rwkv_lnx_g': 1.0 + nrm(ks[16], (L, D_RWKV), 0.02),
        'rwkv_lnx_b': nrm(ks[17], (L, D_RWKV), 0.02),
        'w_out': nrm(ks[18], (L, D_MIX, D_MODEL), D_MIX ** -0.5 * DN_BETA),
        'ln1_g': 1.0 + nrm(ks[19], (L, D_MODEL), 0.02),
        'ln1_b': nrm(ks[20], (L, D_MODEL), 0.02),
        'moe_w_grp': nrm(ks[21], (L, D_MODEL, N_GROUPS), D_MODEL ** -0.5),
        'moe_b_grp': nrm(ks[22], (L, N_GROUPS), 0.01),
        'moe_w_rt': nrm(ks[23], (L, D_MODEL, N_EXPERTS), D_MODEL ** -0.5),
        'moe_b_rt': nrm(ks[24], (L, N_EXPERTS), 0.01),
        'moe_w_gate': nrm(ks[25], (L, N_EXPERTS, D_MODEL, D_EXPERT), D_MODEL ** -0.5),
        'moe_w_up': nrm(ks[26], (L, N_EXPERTS, D_MODEL, D_EXPERT), D_MODEL ** -0.5),
        'moe_w_down': nrm(ks[27], (L, N_EXPERTS, D_EXPERT, D_MODEL), D_EXPERT ** -0.5 * DN_BETA),
        'ple_w_proj': nrm(ks[28], (L, PLE_DIM, D_MODEL), PLE_DIM ** -0.5 * DN_BETA),
        'ple_w_gate_down': nrm(ks[29], (L, D_MODEL, PLE_DIM), D_MODEL ** -0.5),
        'ple_w_gate_up': nrm(ks[30], (L, PLE_DIM, D_MODEL), PLE_DIM ** -0.5),
        'ln2_g': 1.0 + nrm(ks[31], (L, D_MODEL), 0.02),
        'ln2_b': nrm(ks[32], (L, D_MODEL), 0.02),
    }


def reference(x, p, w_in, gdn_conv_w, gdn_a_log, gdn_dt_bias, gdn_norm_w,
              rwkv_mu, rwkv_w0, rwkv_w2, rwkv_a0, rwkv_a2, rwkv_g2, rwkv_k_k, rwkv_k_a,
              rwkv_r_k, rwkv_lnx_g, rwkv_lnx_b, w_out, ln1_g, ln1_b,
              moe_w_grp, moe_b_grp, moe_w_rt, moe_b_rt, moe_w_gate, moe_w_up, moe_w_down,
              ple_w_proj, ple_w_gate_down, ple_w_gate_up, ln2_g, ln2_b):
    for i in range(DEPTH):
        h = x @ w_in[i]
        dq, dk, dv, dz, da, db, h_rwkv = _split(h, [D_DELTA] * 4 + [GDN_HEADS] * 2 + [RWKV_COLS])
        o_gdn = _gated_deltanet_group(dq, dk, dv, dz, da, db, gdn_conv_w[i], gdn_a_log[i],
                                      gdn_dt_bias[i], gdn_norm_w[i])
        o_rwkv = _rwkv7_group(h_rwkv, rwkv_mu[i], rwkv_w0[i], rwkv_w2[i], rwkv_a0[i], rwkv_a2[i],
                              rwkv_g2[i], rwkv_k_k[i], rwkv_k_a[i], rwkv_r_k[i],
                              rwkv_lnx_g[i], rwkv_lnx_b[i])
        mix = jnp.concatenate([o_gdn, o_rwkv], axis=-1) @ w_out[i]
        x = _layer_norm(DN_ALPHA * x + mix, ln1_g[i], ln1_b[i])
        ffn = _hierarchical_moe(x, moe_w_grp[i], moe_b_grp[i], moe_w_rt[i], moe_b_rt[i],
                                moe_w_gate[i], moe_w_up[i], moe_w_down[i])
        ple = (p[i] @ ple_w_proj[i]) * jax.nn.sigmoid((x @ ple_w_gate_down[i]) @ ple_w_gate_up[i])
        x = _layer_norm(DN_ALPHA * x + ffn + ple, ln2_g[i], ln2_b[i])
    return x
```

```python
import functools

import jax
import jax.numpy as jnp
from jax import lax
from jax.experimental import pallas as pl
from jax.experimental.pallas import tpu as pltpu

F32 = jnp.float32
BF16 = jnp.bfloat16

DEPTH = 4
DN_ALPHA = (2 * DEPTH) ** 0.25
CHUNK = 64
LANES = 128
GDN_HEAD_DIM = 128
GDN_CONV = 4
RWKV_HEAD_DIM = 64
RWKV_LNX_EPS = 64e-5
N_GROUPS = 4
EXPERTS_PER_GROUP = 8
N_EXPERTS = N_GROUPS * EXPERTS_PER_GROUP
LORA_PAD = 256
AB_PAD = 256
HALO = 8
MOE_TILE = 256
VMEM_LIMIT = 48 * 1024 * 1024


def _sigmoid(x):
    return 1.0 / (1.0 + jnp.exp(-x))


def _softplus(x):
    return jnp.maximum(x, 0.0) + jnp.log(1.0 + jnp.exp(-jnp.abs(x)))


def _split_bf16(x):
    hi = x.astype(BF16)
    lo = (x - hi.astype(F32)).astype(BF16)
    return hi, lo


def _bmm(a, b):
    return jnp.einsum('nik,nkj->nij', a.astype(BF16), b.astype(BF16), preferred_element_type=F32)


def _bmm_nt(a, b):
    return jnp.einsum('nik,njk->nij', a.astype(BF16), b.astype(BF16), preferred_element_type=F32)


def _bmm3(a, b):
    ah, al = _split_bf16(a)
    bh, bl = _split_bf16(b)
    f = lambda x, y: jnp.einsum('nik,nkj->nij', x, y, preferred_element_type=F32)
    return f(ah, bh) + f(al, bh) + f(ah, bl)


def _dot(a, b):
    return jnp.dot(a.astype(BF16), b.astype(BF16), preferred_element_type=F32)


def _dot_nt(a, b):
    return lax.dot_general(a.astype(BF16), b.astype(BF16), (((1,), (1,)), ((), ())),
                           preferred_element_type=F32)


def _dot_tn(a, b):
    return lax.dot_general(a.astype(BF16), b.astype(BF16), (((0,), (0,)), ((), ())),
                           preferred_element_type=F32)


def _neumann_inverse(m, steps):
    n = m.shape[-1]
    eye = (lax.broadcasted_iota(jnp.int32, (n, n), 0) == lax.broadcasted_iota(jnp.int32, (n, n), 1))
    p = m + eye.astype(F32)[None]
    mp = m
    for _ in range(steps):
        mp = _bmm3(mp, mp)
        p = p + _bmm3(p, mp)
    return p


def _chunk_cumsum(x, chunk):
    row = lax.broadcasted_iota(jnp.int32, x.shape, 0) & (chunk - 1)
    s = 1
    while s < chunk:
        x = x + jnp.where(row >= s, pltpu.roll(x, s, 0), 0.0)
        s *= 2
    return x


def _layer_norm(y, g, b):
    yc = y - jnp.mean(y, axis=-1, keepdims=True)
    var = jnp.mean(yc * yc, axis=-1, keepdims=True)
    return yc * lax.rsqrt(var + 1e-5) * g + b


def _mm_kernel(x_ref, w_ref, o_ref):
    o_ref[...] = jnp.dot(x_ref[...], w_ref[...], preferred_element_type=F32)


def _matmul(x, w, tm=1024, tn=512):
    m, k = x.shape
    n = w.shape[1]
    tm, tn = min(tm, m), min(tn, n)
    return pl.pallas_call(
        _mm_kernel,
        grid=(m // tm, n // tn),
        in_specs=[pl.BlockSpec((tm, k), lambda i, j: (i, 0)),
                  pl.BlockSpec((k, tn), lambda i, j: (0, j))],
        out_specs=pl.BlockSpec((tm, tn), lambda i, j: (i, j)),
        out_shape=jax.ShapeDtypeStruct((m, n), F32),
        compiler_params=pltpu.CompilerParams(
            dimension_semantics=("parallel", "parallel"), vmem_limit_bytes=VMEM_LIMIT),
        name="proj_matmul",
    )(x, w)


def _gdn_kernel(q_ref, k_ref, v_ref, z_ref, ab_ref, cwq_ref, cwk_ref, cwv_ref,
                alog_ref, dtb_ref, nw_ref, o_ref, ext_ref, s_ref, *, tb, n_heads):
    h = pl.program_id(1)
    t_idx = pl.program_id(2)
    nc = tb // CHUNK

    @pl.when(t_idx == 0)
    def _():
        ext_ref[:, 0:HALO, :] = jnp.zeros((3, HALO, LANES), F32)
        s_ref[...] = jnp.zeros_like(s_ref)

    def conv_silu(idx, x_ref, cw_ref):
        ext_ref[idx, HALO:HALO + tb, :] = x_ref[0]
        cw = cw_ref[...]
        acc = ext_ref[idx, HALO - 3:HALO - 3 + tb, :] * cw[0:1]
        for j in range(1, GDN_CONV):
            acc = acc + ext_ref[idx, HALO - 3 + j:HALO - 3 + j + tb, :] * cw[j:j + 1]
        ext_ref[idx, 0:HALO, :] = ext_ref[idx, tb:tb + HALO, :]
        return acc * _sigmoid(acc)

    q = conv_silu(0, q_ref, cwq_ref)
    k = conv_silu(1, k_ref, cwk_ref)
    v = conv_silu(2, v_ref, cwv_ref)
    q = q * lax.rsqrt(jnp.sum(q * q, axis=-1, keepdims=True) + 1e-6) * (GDN_HEAD_DIM ** -0.5)
    k = k * lax.rsqrt(jnp.sum(k * k, axis=-1, keepdims=True) + 1e-6)

    ab = ab_ref[0]
    lane = lax.broadcasted_iota(jnp.int32, ab.shape, 1)
    a_col = jnp.sum(jnp.where(lane == h, ab, 0.0), axis=-1, keepdims=True)
    b_col = jnp.sum(jnp.where(lane == h + n_heads, ab, 0.0), axis=-1, keepdims=True)
    g = -jnp.exp(alog_ref[0]) * _softplus(a_col + dtb_ref[0])
    beta = jnp.broadcast_to(_sigmoid(b_col), g.shape)
    g = _chunk_cumsum(g, CHUNK)

    c3 = lambda t: t.reshape(nc, CHUNK, t.shape[-1])
    q3, k3, v3, g3, beta3 = c3(q), c3(k), c3(v), c3(g), c3(beta)
    g_last = g3[:, CHUNK - 1:CHUNK, :]

    p1 = g3.astype(BF16).astype(F32)
    p2 = (g3 - p1).astype(BF16).astype(F32)
    p3 = (g3 - p1 - p2).astype(BF16).astype(F32)
    lane3 = lax.broadcasted_iota(jnp.int32, g3.shape, 2)
    pick = jnp.where((lane3 == 0) | (lane3 == 3), p1, jnp.where((lane3 == 1) | (lane3 == 4), p2, p3))
    col_op = jnp.where(lane3 < 3, pick, jnp.where(lane3 < 6, 1.0, 0.0))
    row_op = jnp.where(lane3 < 3, 1.0, jnp.where(lane3 < 6, -pick, 0.0))
    gdiff = _bmm_nt(col_op, row_op)

    ri = lax.broadcasted_iota(jnp.int32, (CHUNK, CHUNK), 0)
    ci = lax.broadcasted_iota(jnp.int32, (CHUNK, CHUNK), 1)
    causal = (ri >= ci)[None]
    strict = (ri > ci)[None]
    decay = jnp.where(causal, jnp.exp(jnp.where(causal, gdiff, 0.0)), 0.0)

    kb3 = k3 * beta3
    a_low = jnp.where(strict, _bmm_nt(kb3, k3) * decay, 0.0)
    tinv = _neumann_inverse(-a_low, 5)
    eg = jnp.exp(g3)
    u3 = _bmm3(tinv, v3 * beta3)
    w3 = _bmm3(tinv, kb3 * eg)
    attn = _bmm_nt(q3, k3) * decay
    qd3 = q3 * eg
    kd3 = k3 * jnp.exp(g_last - g3)
    gl3 = jnp.exp(g_last)

    s = s_ref[...]
    outs = []
    for c in range(nc):
        v_new = u3[c] - _dot(w3[c], s)
        outs.append(_dot(qd3[c], s) + _dot(attn[c], v_new))
        s = s * gl3[c] + _dot_tn(kd3[c], v_new)
    s_ref[...] = s
    o = jnp.concatenate(outs, axis=0)

    o = o * lax.rsqrt(jnp.mean(o * o, axis=-1, keepdims=True) + 1e-6) * nw_ref[...]
    z = z_ref[0]
    o_ref[0] = (o * (z * _sigmoid(z))).astype(o_ref.dtype)


def _gdn(h3, conv_w, a_log, dt_bias, norm_w, d_delta, tb):
    b, t, _ = h3.shape
    nh = d_delta // GDN_HEAD_DIM
    tb = min(tb, t)
    hb = d_delta // LANES
    col = lambda off: pl.BlockSpec((1, tb, LANES), lambda bi, hi, ti, off=off: (bi, ti, off + hi))
    cw = lambda off: pl.BlockSpec((GDN_CONV, LANES), lambda bi, hi, ti, off=off: (0, off + hi))
    per_head = pl.BlockSpec((1, 1, LANES), lambda bi, hi, ti: (hi, 0, 0))
    alog_b = jnp.broadcast_to(a_log.astype(F32)[:, None, None], (nh, 1, LANES))
    dtb_b = jnp.broadcast_to(dt_bias.astype(F32)[:, None, None], (nh, 1, LANES))
    return pl.pallas_call(
        functools.partial(_gdn_kernel, tb=tb, n_heads=nh),
        grid=(b, nh, t // tb),
        in_specs=[col(0), col(hb), col(2 * hb), col(3 * hb),
                  pl.BlockSpec((1, tb, LANES), lambda bi, hi, ti: (bi, ti, 4 * hb)),
                  cw(0), cw(hb), cw(2 * hb), per_head, per_head,
                  pl.BlockSpec((1, LANES), lambda bi, hi, ti: (0, 0))],
        out_specs=pl.BlockSpec((1, tb, LANES), lambda bi, hi, ti: (bi, ti, hi)),
        out_shape=jax.ShapeDtypeStruct((b, t, d_delta), BF16),
        scratch_shapes=[pltpu.VMEM((3, HALO + tb, LANES), F32),
                        pltpu.VMEM((GDN_HEAD_DIM, GDN_HEAD_DIM), F32)],
        compiler_params=pltpu.CompilerParams(
            dimension_semantics=("parallel", "parallel", "arbitrary"), vmem_limit_bytes=VMEM_LIMIT),
        name="gdn_chunk",
    )(h3, h3, h3, h3, h3, conv_w, conv_w, conv_w, alog_b, dtb_b, norm_w.reshape(1, LANES))


def _rwkv_kernel(r_ref, k_ref, v_ref, xl_ref, mur_ref, muk_ref, muv_ref, mul_ref,
                 w0_ref, a0_ref, kk_ref, ka_ref, rk_ref, lg_ref, lb_ref,
                 w2_ref, a2_ref, g2_ref, o_ref, ext_ref, extl_ref, s_ref, *, tb):
    t_idx = pl.program_id(2)
    nc = tb // CHUNK
    c2 = 2 * CHUNK

    @pl.when(t_idx == 0)
    def _():
        ext_ref[:, 0:HALO, :] = jnp.zeros((3, HALO, LANES), F32)
        extl_ref[0:HALO, :] = jnp.zeros((HALO, LORA_PAD), F32)
        s_ref[...] = jnp.zeros_like(s_ref)

    def shift(ext, x, mu):
        ext[HALO:HALO + tb, :] = x
        prev = ext[HALO - 1:HALO - 1 + tb, :]
        ext[0:HALO, :] = ext[tb:tb + HALO, :]
        return x + mu * (prev - x)

    r = shift(ext_ref.at[0], r_ref[0], mur_ref[...])
    k = shift(ext_ref.at[1], k_ref[0], muk_ref[...])
    v = shift(ext_ref.at[2], v_ref[0], muv_ref[...])
    xl = shift(extl_ref, xl_ref[0], mul_ref[...])

    w_log = -_softplus(-(w0_ref[...] + _dot(jnp.tanh(xl), w2_ref[...]))) - 0.5
    ld = -jnp.exp(w_log)
    a = _sigmoid(a0_ref[...] + _dot(xl, a2_ref[...]))
    gate = _dot(_sigmoid(xl), g2_ref[...])

    lane = lax.broadcasted_iota(jnp.int32, (tb, LANES), 1)
    head0 = lane < RWKV_HEAD_DIM

    def head_sum(x):
        s0 = jnp.sum(jnp.where(head0, x, 0.0), axis=-1, keepdims=True)
        s1 = jnp.sum(jnp.where(head0, 0.0, x), axis=-1, keepdims=True)
        return jnp.where(head0, s0, s1)

    kk = k * kk_ref[...]
    kk = kk * lax.rsqrt(head_sum(kk * kk) + 1e-6)
    k2 = k * (1.0 + (a - 1.0) * ka_ref[...])
    aa = -kk
    bb = kk * a

    lc = _chunk_cumsum(ld, CHUNK)
    c3 = lambda t: t.reshape(nc, CHUNK, LANES)
    lc3 = c3(lc)
    l_last = lc3[:, CHUNK - 1:CHUNK, :]
    e_neg = jnp.exp(-lc)
    a_t = c3(aa * jnp.exp(lc - ld))
    b_t = c3(bb * e_neg)
    k_t = c3(k2 * e_neg)
    r_t = c3(r * jnp.exp(lc))
    e_tail = jnp.exp(l_last - lc3)
    b_h = c3(bb) * e_tail
    k_h = c3(k2) * e_tail
    w_c = jnp.exp(l_last)
    v3 = c3(v)

    head0_3 = lax.broadcasted_iota(jnp.int32, (nc, CHUNK, LANES), 2) < RWKV_HEAD_DIM
    stack = lambda x: jnp.concatenate([jnp.where(head0_3, x, 0.0), jnp.where(head0_3, 0.0, x)], axis=1)
    dup = lambda x: jnp.concatenate([x, x], axis=1)
    a_s, r_s = stack(a_t), stack(r_t)
    bb2, kk2, vv2 = dup(b_t), dup(k_t), dup(v3)

    ri = lax.broadcasted_iota(jnp.int32, (c2, c2), 0)
    ci = lax.broadcasted_iota(jnp.int32, (c2, c2), 1)
    same = (ri >= CHUNK) == (ci >= CHUNK)
    strict = (same & (ri > ci))[None]
    incl = (same & (ri >= ci))[None]
    m_ab = jnp.where(strict, _bmm_nt(a_s, bb2), 0.0)
    m_ak = jnp.where(strict, _bmm_nt(a_s, kk2), 0.0)
    m_rb = jnp.where(incl, _bmm_nt(r_s, bb2), 0.0)
    m_rk = jnp.where(incl, _bmm_nt(r_s, kk2), 0.0)
    tinv = _neumann_inverse(m_ab, 5)
    p_s = _bmm3(tinv, _bmm(m_ak, vv2))
    t_a = _bmm3(tinv, a_s)
    y_v = _bmm(m_rk, vv2)

    h0c = lax.broadcasted_iota(jnp.int32, (CHUNK, LANES), 1) < RWKV_HEAD_DIM
    bd = ((lax.broadcasted_iota(jnp.int32, (LANES, LANES), 0) >= RWKV_HEAD_DIM)
          == (lax.broadcasted_iota(jnp.int32, (LANES, LANES), 1) >= RWKV_HEAD_DIM))
    s = s_ref[...]
    outs = []
    for c in range(nc):
        u_s = p_s[c] + _dot_nt(t_a[c], s)
        u = jnp.where(h0c, u_s[:CHUNK], u_s[CHUNK:])
        y_s = _dot(m_rb[c], jnp.concatenate([u, u], axis=0)) + y_v[c]
        outs.append(_dot_nt(r_t[c], s) + jnp.where(h0c, y_s[:CHUNK], y_s[CHUNK:]))
        upd = _dot_tn(jnp.concatenate([u, v3[c]], axis=0), jnp.concatenate([b_h[c], k_h[c]], axis=0))
        s = s * w_c[c] + jnp.where(bd, upd, 0.0)
    s_ref[...] = s
    y = jnp.concatenate(outs, axis=0)

    inv_n = 1.0 / RWKV_HEAD_DIM
    yc = y - head_sum(y) * inv_n
    y = yc * lax.rsqrt(head_sum(yc * yc) * inv_n + RWKV_LNX_EPS)
    y = y * lg_ref[...] + lb_ref[...]
    y = y + head_sum(r * k2 * rk_ref[...]) * v
    o_ref[0] = (y * gate).astype(o_ref.dtype)


def _rwkv(h3, col0, d_rwkv, mu, w0, w2p, a0, a2p, g2p, k_k, k_a, r_k, lnx_g, lnx_b, tb):
    b, t, _ = h3.shape
    tb = min(tb, t)
    npair = d_rwkv // LANES
    cb = col0 // LANES
    lora_blk = (col0 + 3 * d_rwkv) // LORA_PAD
    col = lambda off: pl.BlockSpec((1, tb, LANES), lambda bi, hi, ti, off=off: (bi, ti, cb + off + hi))
    vec = lambda off: pl.BlockSpec((1, LANES), lambda bi, hi, ti, off=off: (0, off + hi))
    lw = pl.BlockSpec((LORA_PAD, LANES), lambda bi, hi, ti: (0, hi))
    row = lambda x: x.astype(F32).reshape(1, -1)
    mu2 = row(mu)
    return pl.pallas_call(
        functools.partial(_rwkv_kernel, tb=tb),
        grid=(b, npair, t // tb),
        in_specs=[col(0), col(npair), col(2 * npair),
                  pl.BlockSpec((1, tb, LORA_PAD), lambda bi, hi, ti: (bi, ti, lora_blk)),
                  vec(0), vec(npair), vec(2 * npair),
                  pl.BlockSpec((1, LORA_PAD), lambda bi, hi, ti: (0, 3 * d_rwkv // LORA_PAD)),
                  vec(0), vec(0), vec(0), vec(0), vec(0), vec(0), vec(0),
                  lw, lw, lw],
        out_specs=pl.BlockSpec((1, tb, LANES), lambda bi, hi, ti: (bi, ti, hi)),
        out_shape=jax.ShapeDtypeStruct((b, t, d_rwkv), BF16),
        scratch_shapes=[pltpu.VMEM((3, HALO + tb, LANES), F32),
                        pltpu.VMEM((HALO + tb, LORA_PAD), F32),
                        pltpu.VMEM((LANES, LANES), F32)],
        compiler_params=pltpu.CompilerParams(
            dimension_semantics=("parallel", "parallel", "arbitrary"), vmem_limit_bytes=VMEM_LIMIT),
        name="rwkv_chunk",
    )(h3, h3, h3, h3, mu2, mu2, mu2, mu2, row(w0), row(a0), row(k_k), row(k_a), row(r_k),
      row(lnx_g), row(lnx_b), w2p, a2p, g2p)


def _ln_router_kernel(mix_ref, x_ref, g_ref, b_ref, wh_ref, wl_ref, br_ref, x1_ref, x1b_ref, rt_ref):
    x1 = _layer_norm(DN_ALPHA * x_ref[...] + mix_ref[...], g_ref[...], b_ref[...])
    x1_ref[...] = x1
    x1b_ref[...] = x1.astype(BF16)
    hi, lo = _split_bf16(x1)
    d = lambda a, w: jnp.dot(a, w, preferred_element_type=F32)
    logits = d(hi, wh_ref[...]) + d(lo, wh_ref[...]) + d(hi, wl_ref[...]) + br_ref[...]

    lane_i = lax.broadcasted_iota(jnp.int32, logits.shape, 1)
    lane = lane_i.astype(F32)
    big = float(LANES)
    neg = -jnp.inf
    gmask = lane_i < N_GROUPS
    lg = jnp.where(gmask, logits, neg)
    m = jnp.max(lg, axis=-1, keepdims=True)
    p_grp = 1.0 / jnp.sum(jnp.where(gmask, jnp.exp(lg - m), 0.0), axis=-1, keepdims=True)
    g_sel = jnp.min(jnp.where(gmask & (lg == m), lane, big), axis=-1, keepdims=True)
    lane_grp = lax.shift_right_arithmetic(lane_i - N_GROUPS, EXPERTS_PER_GROUP.bit_length() - 1)
    emask = (lane_i >= N_GROUPS) & (lane_i < N_GROUPS + N_EXPERTS) & (lane_grp == g_sel.astype(jnp.int32))
    le = jnp.where(emask, logits, neg)
    v1 = jnp.max(le, axis=-1, keepdims=True)
    i1 = jnp.min(jnp.where(emask & (le == v1), lane, big), axis=-1, keepdims=True)
    emask2 = emask & (lane != i1)
    le2 = jnp.where(emask2, logits, neg)
    v2 = jnp.max(le2, axis=-1, keepdims=True)
    i2 = jnp.min(jnp.where(emask2 & (le2 == v2), lane, big), axis=-1, keepdims=True)
    t = jnp.exp(v2 - v1)
    den = 1.0 + t
    gate1 = p_grp / den
    gate2 = p_grp * t / den
    rt_ref[...] = jnp.where(lane_i == 0, i1 - N_GROUPS,
                            jnp.where(lane_i == 1, i2 - N_GROUPS,
                                      jnp.where(lane_i == 2, gate1, jnp.where(lane_i == 3, gate2, 0.0))))


def _ln_router(mix, x, g, b, wr_hi, wr_lo, br, tm=256):
    m, d = x.shape
    tm = min(tm, m)
    rows = lambda w: pl.BlockSpec((tm, w), lambda i: (i, 0))
    full = lambda s: pl.BlockSpec(s, lambda i: (0, 0))
    return pl.pallas_call(
        _ln_router_kernel,
        grid=(m // tm,),
        in_specs=[rows(d), rows(d), full((1, d)), full((1, d)),
                  full((d, LANES)), full((d, LANES)), full((1, LANES))],
        out_specs=[rows(d), rows(d), rows(LANES)],
        out_shape=[jax.ShapeDtypeStruct((m, d), F32), jax.ShapeDtypeStruct((m, d), BF16),
                   jax.ShapeDtypeStruct((m, LANES), F32)],
        compiler_params=pltpu.CompilerParams(
            dimension_semantics=("parallel",), vmem_limit_bytes=VMEM_LIMIT),
        name="ln_router",
    )(mix, x, g.reshape(1, d), b.reshape(1, d), wr_hi, wr_lo, br)


def _moe_kernel(te_ref, nt_ref, src_ref, dst_ref, x_hbm, gate_ref, wg_ref, wu_ref, wd_ref,
                y_hbm, xbuf, ybuf, sem_in, sem_out, *, tm):
    i = pl.program_id(0)

    def row_in(t, r):
        return pltpu.make_async_copy(x_hbm.at[pl.ds(t, 1)], xbuf.at[pl.ds(r, 1)], sem_in)

    def row_out(r, d):
        return pltpu.make_async_copy(ybuf.at[pl.ds(r, 1)], y_hbm.at[pl.ds(d, 1)], sem_out)

    @pl.when(i < nt_ref[0])
    def _():
        base = i * tm

        def gather(r, c):
            row_in(src_ref[base + r], r).start()
            return c

        def gather_wait(r, c):
            row_in(0, r).wait()
            return c

        lax.fori_loop(0, tm, gather, 0)
        lax.fori_loop(0, tm, gather_wait, 0)

        xb = xbuf[...].astype(BF16)
        hg = jnp.dot(xb, wg_ref[0], preferred_element_type=F32)
        hu = jnp.dot(xb, wu_ref[0], preferred_element_type=F32)
        hact = hg * _sigmoid(hg) * hu * gate_ref[...]
        ybuf[...] = jnp.dot(hact.astype(BF16), wd_ref[0], preferred_element_type=F32)

        def scatter(r, c):
            d = dst_ref[base + r]

            @pl.when(d >= 0)
            def _():
                row_out(r, d).start()
            return c

        def scatter_wait(r, c):
            d = dst_ref[base + r]

            @pl.when(d >= 0)
            def _():
                row_out(r, d).wait()
            return c

        lax.fori_loop(0, tm, scatter, 0)
        lax.fori_loop(0, tm, scatter_wait, 0)


def _moe(x1, route, wg, wu, wd, tm):
    m, d = x1.shape
    n_e, _, f = wg.shape
    tm = min(tm, m)
    pairs = 2 * m
    nt_max = pairs // tm + n_e
    rows = nt_max * tm

    e_flat = jnp.concatenate([route[:, 0], route[:, 1]]).astype(jnp.int32)
    g_flat = jnp.concatenate([route[:, 2], route[:, 3]])
    cnt = jnp.sum(e_flat[:, None] == jnp.arange(n_e, dtype=jnp.int32)[None, :], axis=0, dtype=jnp.int32)
    cnt_pad = (cnt + tm - 1) // tm * tm
    end_pad = jnp.cumsum(cnt_pad)
    off_pad = end_pad - cnt_pad
    start = jnp.cumsum(cnt) - cnt
    order = jnp.argsort(e_flat, stable=True).astype(jnp.int32)
    e_sorted = e_flat[order]
    dest = off_pad[e_sorted] + jnp.arange(pairs, dtype=jnp.int32) - start[e_sorted]
    row_pair = jnp.full((rows,), -1, jnp.int32).at[dest].set(order)
    valid = row_pair >= 0
    src_tok = jnp.where(valid, row_pair % m, 0)
    row_gate = jnp.where(valid, g_flat[jnp.maximum(row_pair, 0)], 0.0).reshape(rows, 1)
    n_tiles = (end_pad[-1] // tm).astype(jnp.int32).reshape(1)
    tile_e = jnp.searchsorted(end_pad, jnp.arange(nt_max, dtype=jnp.int32) * tm, side='right')
    tile_e = jnp.minimum(tile_e, n_e - 1).astype(jnp.int32)

    grid_spec = pltpu.PrefetchScalarGridSpec(
        num_scalar_prefetch=4,
        grid=(nt_max,),
        in_specs=[pl.BlockSpec(memory_space=pl.ANY),
                  pl.BlockSpec((tm, 1), lambda i, te, nt, s, dd: (i, 0)),
                  pl.BlockSpec((1, d, f), lambda i, te, nt, s, dd: (te[i], 0, 0)),
                  pl.BlockSpec((1, d, f), lambda i, te, nt, s, dd: (te[i], 0, 0)),
                  pl.BlockSpec((1, f, d), lambda i, te, nt, s, dd: (te[i], 0, 0))],
        out_specs=pl.BlockSpec(memory_space=pl.ANY),
        scratch_shapes=[pltpu.VMEM((tm, d), F32), pltpu.VMEM((tm, d), F32),
                        pltpu.SemaphoreType.DMA, pltpu.SemaphoreType.DMA],
    )
    return pl.pallas_call(
        functools.partial(_moe_kernel, tm=tm),
        grid_spec=grid_spec,
        out_shape=jax.ShapeDtypeStruct((pairs, d), F32),
        compiler_params=pltpu.CompilerParams(
            dimension_semantics=("arbitrary",), vmem_limit_bytes=VMEM_LIMIT),
        name="moe_experts",
    )(tile_e, n_tiles, src_tok, row_pair, x1, row_gate, wg, wu, wd)


def _ple_ln_kernel(x1_ref, x1b_ref, y0_ref, y1_ref, p_ref, wp_ref, gd_ref, gu_ref, g_ref, b_ref,
                   x2_ref, x2b_ref):
    d = lambda a, w: jnp.dot(a, w, preferred_element_type=F32)
    proj = d(p_ref[...].astype(BF16), wp_ref[...])
    gate = _sigmoid(d(d(x1b_ref[...], gd_ref[...]).astype(BF16), gu_ref[...]))
    y = DN_ALPHA * x1_ref[...] + (y0_ref[...] + y1_ref[...]) + proj * gate
    x2 = _layer_norm(y, g_ref[...], b_ref[...])
    x2_ref[...] = x2
    x2b_ref[...] = x2.astype(BF16)


def _ple_ln(x1, x1b, y, p, wp, gd, gu, g, b, tm=128):
    m, d = x1.shape
    pd = p.shape[1]
    tm = min(tm, m)
    nb = m // tm
    rows = lambda w: pl.BlockSpec((tm, w), lambda i: (i, 0))
    full = lambda s: pl.BlockSpec(s, lambda i: (0, 0))
    return pl.pallas_call(
        _ple_ln_kernel,
        grid=(nb,),
        in_specs=[rows(d), rows(d), rows(d), pl.BlockSpec((tm, d), lambda i: (i + nb, 0)), rows(pd),
                  full((pd, d)), full((d, pd)), full((pd, d)), full((1, d)), full((1, d))],
        out_specs=[rows(d), rows(d)],
        out_shape=[jax.ShapeDtypeStruct((m, d), F32), jax.ShapeDtypeStruct((m, d), BF16)],
        compiler_params=pltpu.CompilerParams(
            dimension_semantics=("parallel",), vmem_limit_bytes=VMEM_LIMIT),
        name="ple_ln",
    )(x1, x1b, y, y, p, wp, gd, gu, g.reshape(1, d), b.reshape(1, d))


def _pad_rows(w, rows, offset):
    out = jnp.zeros((rows, w.shape[1]), F32)
    return out.at[offset:offset + w.shape[0]].set(w.astype(F32)).astype(BF16)


def _layer(x, xb, p, w_in, conv_w, a_log, dt_bias, norm_w, mu, w0, w2, a0, a2, g2, k_k, k_a, r_k,
           lnx_g, lnx_b, w_out, ln1_g, ln1_b, w_grp, b_grp, w_rt, b_rt, wg, wu, wd,
           wp, gd, gu, ln2_g, ln2_b, *, batch, tb_gdn=512, tb_rwkv=256):
    m, d_model = x.shape
    t = m // batch
    d_mix = w_out.shape[0]
    d_delta = d_mix // 2
    d_rwkv = d_mix - d_delta
    nh = d_delta // GDN_HEAD_DIM
    n_ab = 2 * nh
    lora_w, lora_a, lora_g = w2.shape[0], a2.shape[0], g2.shape[0]

    w_main = w_in[:, :4 * d_delta]
    w_ab = w_in[:, 4 * d_delta:4 * d_delta + n_ab]
    w_rw = w_in[:, 4 * d_delta + n_ab:]
    w_p = jnp.concatenate([w_main, w_ab, jnp.zeros((d_model, AB_PAD - n_ab), w_in.dtype), w_rw],
                          axis=1).astype(BF16)
    col_rwkv = 4 * d_delta + AB_PAD

    h = _matmul(xb, w_p)
    h3 = h.reshape(batch, t, h.shape[1])
    o_gdn = _gdn(h3, conv_w, a_log, dt_bias, norm_w, d_delta, tb_gdn)
    o_rwkv = _rwkv(h3, col_rwkv, d_rwkv, mu, w0,
                   _pad_rows(w2, LORA_PAD, 0), a0, _pad_rows(a2, LORA_PAD, lora_w),
                   _pad_rows(g2, LORA_PAD, lora_w + lora_a),
                   k_k, k_a, r_k, lnx_g, lnx_b, tb_rwkv)
    mix_in = jnp.concatenate([o_gdn, o_rwkv], axis=-1).reshape(m, d_mix)
    mix = _matmul(mix_in, w_out.astype(BF16))

    w_router = jnp.concatenate(
        [w_grp, w_rt, jnp.zeros((d_model, LANES - N_GROUPS - N_EXPERTS), F32)], axis=1).astype(F32)
    wr_hi, wr_lo = _split_bf16(w_router)
    b_router = jnp.concatenate(
        [b_grp, b_rt, jnp.zeros((LANES - N_GROUPS - N_EXPERTS,), F32)]).astype(F32).reshape(1, LANES)
    x1, x1b, route = _ln_router(mix, x, ln1_g, ln1_b, wr_hi, wr_lo, b_router)

    y = _moe(x1, route, wg.astype(BF16), wu.astype(BF16), wd.astype(BF16), MOE_TILE)
    return _ple_ln(x1, x1b, y, p, wp.astype(BF16), gd.astype(BF16), gu.astype(BF16), ln2_g, ln2_b)


def kernel(x, p, w_in, gdn_conv_w, gdn_a_log, gdn_dt_bias, gdn_norm_w, rwkv_mu, rwkv_w0, rwkv_w2, rwkv_a0, rwkv_a2, rwkv_g2, rwkv_k_k, rwkv_k_a, rwkv_r_k, rwkv_lnx_g, rwkv_lnx_b, w_out, ln1_g, ln1_b, moe_w_grp, moe_b_grp, moe_w_rt, moe_b_rt, moe_w_gate, moe_w_up, moe_w_down, ple_w_proj, ple_w_gate_down, ple_w_gate_up, ln2_g, ln2_b):
    batch, t, d_model = x.shape
    m = batch * t
    xf = x.reshape(m, d_model).astype(F32)
    xb = xf.astype(BF16)
    for i in range(w_in.shape[0]):
        xf, xb = _layer(
            xf, xb, p[i].reshape(m, -1), w_in[i], gdn_conv_w[i], gdn_a_log[i], gdn_dt_bias[i],
            gdn_norm_w[i], rwkv_mu[i], rwkv_w0[i], rwkv_w2[i], rwkv_a0[i], rwkv_a2[i], rwkv_g2[i],
            rwkv_k_k[i], rwkv_k_a[i], rwkv_r_k[i], rwkv_lnx_g[i], rwkv_lnx_b[i], w_out[i],
            ln1_g[i], ln1_b[i], moe_w_grp[i], moe_b_grp[i], moe_w_rt[i], moe_b_rt[i],
            moe_w_gate[i], moe_w_up[i], moe_w_down[i], ple_w_proj[i], ple_w_gate_down[i],
            ple_w_gate_up[i], ln2_g[i], ln2_b[i], batch=batch)
    return xf.reshape(batch, t, d_model)
```

```python
import functools

import jax
import jax.numpy as jnp
from jax import lax
from jax.experimental import pallas as pl
from jax.experimental.pallas import tpu as pltpu

F32 = jnp.float32
BF16 = jnp.bfloat16

DEPTH = 4
DN_ALPHA = (2 * DEPTH) ** 0.25
CHUNK = 64
LANES = 128
GDN_HEAD_DIM = 128
GDN_CONV = 4
RWKV_HEAD_DIM = 64
RWKV_LNX_EPS = 64e-5
N_GROUPS = 4
EXPERTS_PER_GROUP = 8
N_EXPERTS = N_GROUPS * EXPERTS_PER_GROUP
LORA_PAD = 256
AB_PAD = 256
HALO = 8
MOE_TILE = 256
STREAMS = 2
VMEM_LIMIT = 48 * 1024 * 1024


def _sigmoid(x):
    return 1.0 / (1.0 + jnp.exp(-x))


def _softplus(x):
    return jnp.maximum(x, 0.0) + jnp.log(1.0 + jnp.exp(-jnp.abs(x)))


def _split_bf16(x):
    hi = x.astype(BF16)
    lo = (x - hi.astype(F32)).astype(BF16)
    return hi, lo


def _bmm(a, b):
    return jnp.einsum('nik,nkj->nij', a.astype(BF16), b.astype(BF16), preferred_element_type=F32)


def _bmm_nt(a, b):
    return jnp.einsum('nik,njk->nij', a.astype(BF16), b.astype(BF16), preferred_element_type=F32)


def _dot(a, b):
    return jnp.dot(a.astype(BF16), b.astype(BF16), preferred_element_type=F32)


def _dot_tn(a, b):
    return lax.dot_general(a.astype(BF16), b.astype(BF16), (((0,), (0,)), ((), ())),
                           preferred_element_type=F32)


def _neumann_inverse(m, steps):
    n = m.shape[-1]
    eye = (lax.broadcasted_iota(jnp.int32, (n, n), 0) == lax.broadcasted_iota(jnp.int32, (n, n), 1))
    p = m + eye.astype(F32)[None]
    mp = _bmm(m, m)
    for _ in range(steps - 1):
        both = _bmm(jnp.concatenate([p, mp], axis=1), mp)
        p = p + both[:, :n]
        mp = both[:, n:]
    return p + _bmm(p, mp)


def _chunk_cumsum(x, chunk):
    row = lax.broadcasted_iota(jnp.int32, x.shape, 0) & (chunk - 1)
    s = 1
    while s < chunk:
        x = x + jnp.where(row >= s, pltpu.roll(x, s, 0), 0.0)
        s *= 2
    return x


def _layer_norm(y, g, b):
    yc = y - jnp.mean(y, axis=-1, keepdims=True)
    var = jnp.mean(yc * yc, axis=-1, keepdims=True)
    return yc * lax.rsqrt(var + 1e-5) * g + b


def _lane_slab(x, j):
    return x[..., j * LANES:(j + 1) * LANES]


def _mm_kernel(x_ref, w_ref, o_ref):
    o_ref[...] = jnp.dot(x_ref[...], w_ref[...], preferred_element_type=F32)


def _matmul(x, w, tm=1024, tn=512):
    m, k = x.shape
    n = w.shape[1]
    tm, tn = min(tm, m), min(tn, n)
    return pl.pallas_call(
        _mm_kernel,
        grid=(m // tm, n // tn),
        in_specs=[pl.BlockSpec((tm, k), lambda i, j: (i, 0)),
                  pl.BlockSpec((k, tn), lambda i, j: (0, j))],
        out_specs=pl.BlockSpec((tm, tn), lambda i, j: (i, j)),
        out_shape=jax.ShapeDtypeStruct((m, n), F32),
        compiler_params=pltpu.CompilerParams(
            dimension_semantics=("parallel", "parallel"), vmem_limit_bytes=VMEM_LIMIT),
        name="proj_matmul",
    )(x, w)


def _gdn_kernel(q_ref, k_ref, v_ref, z_ref, ab_ref, cwq_ref, cwk_ref, cwv_ref,
                alog_ref, dtb_ref, nw_ref, o_ref, ext_ref, s_ref, *, tb, n_heads, group):
    hg = pl.program_id(1)
    t_idx = pl.program_id(2)
    nc = tb // CHUNK
    width = group * LANES

    @pl.when(t_idx == 0)
    def _():
        ext_ref[:, 0:HALO, :] = jnp.zeros((3, HALO, width), F32)
        s_ref[...] = jnp.zeros_like(s_ref)

    def conv_silu(idx, x_ref, cw_ref):
        ext_ref[idx, HALO:HALO + tb, :] = x_ref[0]
        cw = cw_ref[...]
        acc = ext_ref[idx, HALO - 3:HALO - 3 + tb, :] * cw[0:1]
        for j in range(1, GDN_CONV):
            acc = acc + ext_ref[idx, HALO - 3 + j:HALO - 3 + j + tb, :] * cw[j:j + 1]
        ext_ref[idx, 0:HALO, :] = ext_ref[idx, tb:tb + HALO, :]
        return acc * _sigmoid(acc)

    q_all = conv_silu(0, q_ref, cwq_ref)
    k_all = conv_silu(1, k_ref, cwk_ref)
    v_all = conv_silu(2, v_ref, cwv_ref)
    ab = ab_ref[0]
    lane = lax.broadcasted_iota(jnp.int32, ab.shape, 1)
    ri = lax.broadcasted_iota(jnp.int32, (CHUNK, CHUNK), 0)
    ci = lax.broadcasted_iota(jnp.int32, (CHUNK, CHUNK), 1)
    causal = (ri >= ci)[None]
    strict = (ri > ci)[None]
    lane3 = lax.broadcasted_iota(jnp.int32, (nc, CHUNK, LANES), 2)
    c3 = lambda t: t.reshape(nc, CHUNK, t.shape[-1])

    streams = []
    for j in range(group):
        h = hg * group + j
        q = _lane_slab(q_all, j)
        k = _lane_slab(k_all, j)
        q = q * lax.rsqrt(jnp.sum(q * q, axis=-1, keepdims=True) + 1e-6) * (GDN_HEAD_DIM ** -0.5)
        k = k * lax.rsqrt(jnp.sum(k * k, axis=-1, keepdims=True) + 1e-6)
        a_col = jnp.sum(jnp.where(lane == h, ab, 0.0), axis=-1, keepdims=True)
        b_col = jnp.sum(jnp.where(lane == h + n_heads, ab, 0.0), axis=-1, keepdims=True)
        g = -jnp.exp(alog_ref[j]) * _softplus(a_col + dtb_ref[j])
        beta = jnp.broadcast_to(_sigmoid(b_col), g.shape)
        g = _chunk_cumsum(g, CHUNK)
        q3, k3, v3, g3, beta3 = c3(q), c3(k), c3(_lane_slab(v_all, j)), c3(g), c3(beta)
        g_last = g3[:, CHUNK - 1:CHUNK, :]

        p1 = g3.astype(BF16).astype(F32)
        p2 = (g3 - p1).astype(BF16).astype(F32)
        p3 = (g3 - p1 - p2).astype(BF16).astype(F32)
        pick = jnp.where((lane3 == 0) | (lane3 == 3), p1, jnp.where((lane3 == 1) | (lane3 == 4), p2, p3))
        col_op = jnp.where(lane3 < 3, pick, jnp.where(lane3 < 6, 1.0, 0.0))
        row_op = jnp.where(lane3 < 3, 1.0, jnp.where(lane3 < 6, -pick, 0.0))
        gdiff = _bmm_nt(col_op, row_op)
        decay = jnp.where(causal, jnp.exp(jnp.where(causal, gdiff, 0.0)), 0.0)

        kb3 = k3 * beta3
        kq_k = _bmm_nt(jnp.concatenate([kb3, q3], axis=1), k3)
        a_low = jnp.where(strict, kq_k[:, :CHUNK] * decay, 0.0)
        tinv = _neumann_inverse(-a_low, 5)
        eg = jnp.exp(g3)
        uw = _bmm(tinv, jnp.concatenate([v3 * beta3, kb3 * eg], axis=-1))
        streams.append(dict(
            u=uw[..., :GDN_HEAD_DIM], wq=jnp.concatenate([uw[..., GDN_HEAD_DIM:], q3 * eg], axis=1),
            attn=kq_k[:, CHUNK:] * decay, kd=k3 * jnp.exp(g_last - g3), gl=jnp.exp(g_last),
            s=s_ref[j], outs=[]))

    for c in range(nc):
        for st in streams:
            s = st['s']
            wq_s = _dot(st['wq'][c], s)
            v_new = st['u'][c] - wq_s[:CHUNK]
            st['outs'].append(wq_s[CHUNK:] + _dot(st['attn'][c], v_new))
            st['s'] = s * st['gl'][c] + _dot_tn(st['kd'][c], v_new)

    for j, st in enumerate(streams):
        s_ref[j] = st['s']
        o = jnp.concatenate(st['outs'], axis=0)
        o = o * lax.rsqrt(jnp.mean(o * o, axis=-1, keepdims=True) + 1e-6) * nw_ref[...]
        z = _lane_slab(z_ref[0], j)
        o_ref[0, :, j * LANES:(j + 1) * LANES] = (o * (z * _sigmoid(z))).astype(o_ref.dtype)


def _gdn(h3, conv_w, a_log, dt_bias, norm_w, d_delta, tb, group=STREAMS):
    b, t, _ = h3.shape
    nh = d_delta // GDN_HEAD_DIM
    tb = min(tb, t)
    group = min(group, nh)
    width = group * LANES
    hb = d_delta // width
    col = lambda off: pl.BlockSpec((1, tb, width), lambda bi, hi, ti, off=off: (bi, ti, off + hi))
    cw = lambda off: pl.BlockSpec((GDN_CONV, width), lambda bi, hi, ti, off=off: (0, off + hi))
    per_head = pl.BlockSpec((group, 1, LANES), lambda bi, hi, ti: (hi, 0, 0))
    alog_b = jnp.broadcast_to(a_log.astype(F32)[:, None, None], (nh, 1, LANES))
    dtb_b = jnp.broadcast_to(dt_bias.astype(F32)[:, None, None], (nh, 1, LANES))
    return pl.pallas_call(
        functools.partial(_gdn_kernel, tb=tb, n_heads=nh, group=group),
        grid=(b, nh // group, t // tb),
        in_specs=[col(0), col(hb), col(2 * hb), col(3 * hb),
                  pl.BlockSpec((1, tb, LANES), lambda bi, hi, ti: (bi, ti, 4 * d_delta // LANES)),
                  cw(0), cw(hb), cw(2 * hb), per_head, per_head,
                  pl.BlockSpec((1, LANES), lambda bi, hi, ti: (0, 0))],
        out_specs=pl.BlockSpec((1, tb, width), lambda bi, hi, ti: (bi, ti, hi)),
        out_shape=jax.ShapeDtypeStruct((b, t, d_delta), BF16),
        scratch_shapes=[pltpu.VMEM((3, HALO + tb, width), F32),
                        pltpu.VMEM((group, GDN_HEAD_DIM, GDN_HEAD_DIM), F32)],
        compiler_params=pltpu.CompilerParams(
            dimension_semantics=("parallel", "parallel", "arbitrary"), vmem_limit_bytes=VMEM_LIMIT),
        name="gdn_chunk",
    )(h3, h3, h3, h3, h3, conv_w, conv_w, conv_w, alog_b, dtb_b, norm_w.reshape(1, LANES))


def _rwkv_kernel(r_ref, k_ref, v_ref, xl_ref, mur_ref, muk_ref, muv_ref, mul_ref,
                 w0_ref, a0_ref, kk_ref, ka_ref, rk_ref, lg_ref, lb_ref,
                 w2_ref, a2_ref, g2_ref, o_ref, ext_ref, extl_ref, s_ref, *, tb, group):
    t_idx = pl.program_id(2)
    nc = tb // CHUNK
    c2 = 2 * CHUNK
    width = group * LANES

    @pl.when(t_idx == 0)
    def _():
        ext_ref[:, 0:HALO, :] = jnp.zeros((3, HALO, width), F32)
        extl_ref[0:HALO, :] = jnp.zeros((HALO, LORA_PAD), F32)
        s_ref[...] = jnp.zeros_like(s_ref)

    def shift(ext, x, mu):
        ext[HALO:HALO + tb, :] = x
        prev = ext[HALO - 1:HALO - 1 + tb, :]
        ext[0:HALO, :] = ext[tb:tb + HALO, :]
        return x + mu * (prev - x)

    r = shift(ext_ref.at[0], r_ref[0], mur_ref[...])
    k = shift(ext_ref.at[1], k_ref[0], muk_ref[...])
    v = shift(ext_ref.at[2], v_ref[0], muv_ref[...])
    xl = shift(extl_ref, xl_ref[0], mul_ref[...])

    w_log = -_softplus(-(w0_ref[...] + _dot(jnp.tanh(xl), w2_ref[...]))) - 0.5
    ld = -jnp.exp(w_log)
    a = _sigmoid(a0_ref[...] + _dot(xl, a2_ref[...]))
    gate = _dot(_sigmoid(xl), g2_ref[...])

    head0 = lax.broadcasted_iota(jnp.int32, (tb, LANES), 1) < RWKV_HEAD_DIM

    def head_sum(x):
        parts = []
        for j in range(group):
            xj = _lane_slab(x, j)
            s0 = jnp.sum(jnp.where(head0, xj, 0.0), axis=-1, keepdims=True)
            s1 = jnp.sum(jnp.where(head0, 0.0, xj), axis=-1, keepdims=True)
            parts.append(jnp.where(head0, s0, s1))
        return jnp.concatenate(parts, axis=-1) if group > 1 else parts[0]

    kk = k * kk_ref[...]
    kk = kk * lax.rsqrt(head_sum(kk * kk) + 1e-6)
    k2 = k * (1.0 + (a - 1.0) * ka_ref[...])
    bb = kk * a

    lc = _chunk_cumsum(ld, CHUNK)
    c3 = lambda t: t.reshape(nc, CHUNK, t.shape[-1])
    lc3 = c3(lc)
    l_last = lc3[:, CHUNK - 1:CHUNK, :]
    e_neg = jnp.exp(-lc)
    a_t = c3(-kk * jnp.exp(lc - ld))
    b_t = c3(bb * e_neg)
    k_t = c3(k2 * e_neg)
    r_t = c3(r * jnp.exp(lc))
    e_tail = jnp.exp(l_last - lc3)
    b_h = c3(bb) * e_tail
    k_h = c3(k2) * e_tail
    w_c = jnp.exp(l_last)
    v3 = c3(v)

    head0_3 = lax.broadcasted_iota(jnp.int32, (nc, CHUNK, LANES), 2) < RWKV_HEAD_DIM
    stack = lambda x: jnp.concatenate([jnp.where(head0_3, x, 0.0), jnp.where(head0_3, 0.0, x)], axis=1)
    dup = lambda x: jnp.concatenate([x, x], axis=1)
    ri = lax.broadcasted_iota(jnp.int32, (c2, c2), 0)
    ci = lax.broadcasted_iota(jnp.int32, (c2, c2), 1)
    same = (ri >= CHUNK) == (ci >= CHUNK)
    strict = (same & (ri > ci))[None]
    incl = (same & (ri >= ci))[None]
    h0c = lax.broadcasted_iota(jnp.int32, (CHUNK, LANES), 1) < RWKV_HEAD_DIM
    bd = ((lax.broadcasted_iota(jnp.int32, (LANES, LANES), 0) >= RWKV_HEAD_DIM)
          == (lax.broadcasted_iota(jnp.int32, (LANES, LANES), 1) >= RWKV_HEAD_DIM))

    streams = []
    for j in range(group):
        sl = lambda x: _lane_slab(x, j)
        a_s, r_s = stack(sl(a_t)), stack(sl(r_t))
        vv2 = dup(sl(v3))
        ar_bk = _bmm_nt(jnp.concatenate([a_s, r_s], axis=1),
                        jnp.concatenate([dup(sl(b_t)), dup(sl(k_t))], axis=1))
        m_ab = jnp.where(strict, ar_bk[:, :c2, :c2], 0.0)
        m_ak = jnp.where(strict, ar_bk[:, :c2, c2:], 0.0)
        m_rb = jnp.where(incl, ar_bk[:, c2:, :c2], 0.0)
        m_rk = jnp.where(incl, ar_bk[:, c2:, c2:], 0.0)
        tinv = _neumann_inverse(m_ab, 5)
        mv = _bmm(jnp.concatenate([m_ak, m_rk], axis=1), vv2)
        pt = _bmm(tinv, jnp.concatenate([mv[:, :c2], a_s], axis=-1))
        w_col = [jnp.broadcast_to(sl(w_c)[c], (LANES, LANES)).T for c in range(nc)]
        streams.append(dict(
            p=pt[..., :LANES], tr=jnp.concatenate([pt[..., LANES:], sl(r_t)], axis=1), yv=mv[:, c2:],
            m_rb=m_rb, v=sl(v3), bk=jnp.concatenate([sl(b_h), sl(k_h)], axis=1), w_col=w_col,
            s=s_ref[j], outs=[]))

    for c in range(nc):
        for st in streams:
            s = st['s']
            tr_s = _dot(st['tr'][c], s)
            u_s = st['p'][c] + tr_s[:c2]
            u = jnp.where(h0c, u_s[:CHUNK], u_s[CHUNK:])
            y_s = _dot(st['m_rb'][c], jnp.concatenate([u, u], axis=0)) + st['yv'][c]
            st['outs'].append(tr_s[c2:] + jnp.where(h0c, y_s[:CHUNK], y_s[CHUNK:]))
            upd = _dot_tn(st['bk'][c], jnp.concatenate([u, st['v'][c]], axis=0))
            st['s'] = s * st['w_col'][c] + jnp.where(bd, upd, 0.0)

    for j, st in enumerate(streams):
        s_ref[j] = st['s']
    y = jnp.concatenate([jnp.concatenate(st['outs'], axis=0) for st in streams], axis=-1)

    inv_n = 1.0 / RWKV_HEAD_DIM
    yc = y - head_sum(y) * inv_n
    y = yc * lax.rsqrt(head_sum(yc * yc) * inv_n + RWKV_LNX_EPS)
    y = y * lg_ref[...] + lb_ref[...]
    y = y + head_sum(r * k2 * rk_ref[...]) * v
    o_ref[0] = (y * gate).astype(o_ref.dtype)


def _rwkv(h3, col0, d_rwkv, mu, w0, w2p, a0, a2p, g2p, k_k, k_a, r_k, lnx_g, lnx_b, tb, group=STREAMS):
    b, t, _ = h3.shape
    tb = min(tb, t)
    group = min(group, d_rwkv // LANES)
    width = group * LANES
    nblk = d_rwkv // width
    cb = col0 // width
    lora_blk = (col0 + 3 * d_rwkv) // LORA_PAD
    col = lambda off: pl.BlockSpec((1, tb, width), lambda bi, hi, ti, off=off: (bi, ti, cb + off + hi))
    vec = lambda off: pl.BlockSpec((1, width), lambda bi, hi, ti, off=off: (0, off + hi))
    lw = pl.BlockSpec((LORA_PAD, width), lambda bi, hi, ti: (0, hi))
    row = lambda x: x.astype(F32).reshape(1, -1)
    mu2 = row(mu)
    return pl.pallas_call(
        functools.partial(_rwkv_kernel, tb=tb, group=group),
        grid=(b, nblk, t // tb),
        in_specs=[col(0), col(nblk), col(2 * nblk),
                  pl.BlockSpec((1, tb, LORA_PAD), lambda bi, hi, ti: (bi, ti, lora_blk)),
                  vec(0), vec(nblk), vec(2 * nblk),
                  pl.BlockSpec((1, LORA_PAD), lambda bi, hi, ti: (0, 3 * d_rwkv // LORA_PAD)),
                  vec(0), vec(0), vec(0), vec(0), vec(0), vec(0), vec(0),
                  lw, lw, lw],
        out_specs=pl.BlockSpec((1, tb, width), lambda bi, hi, ti: (bi, ti, hi)),
        out_shape=jax.ShapeDtypeStruct((b, t, d_rwkv), BF16),
        scratch_shapes=[pltpu.VMEM((3, HALO + tb, width), F32),
                        pltpu.VMEM((HALO + tb, LORA_PAD), F32),
                        pltpu.VMEM((group, LANES, LANES), F32)],
        compiler_params=pltpu.CompilerParams(
            dimension_semantics=("parallel", "parallel", "arbitrary"), vmem_limit_bytes=VMEM_LIMIT),
        name="rwkv_chunk",
    )(h3, h3, h3, h3, mu2, mu2, mu2, mu2, row(w0), row(a0), row(k_k), row(k_a), row(r_k),
      row(lnx_g), row(lnx_b), w2p, a2p, g2p)


def _ln_router_kernel(mix_ref, x_ref, g_ref, b_ref, wh_ref, wl_ref, br_ref, x1_ref, x1b_ref, rt_ref):
    x1 = _layer_norm(DN_ALPHA * x_ref[...] + mix_ref[...], g_ref[...], b_ref[...])
    x1_ref[...] = x1
    x1b_ref[...] = x1.astype(BF16)
    hi, lo = _split_bf16(x1)
    d = lambda a, w: jnp.dot(a, w, preferred_element_type=F32)
    logits = d(hi, wh_ref[...]) + d(lo, wh_ref[...]) + d(hi, wl_ref[...]) + br_ref[...]

    lane_i = lax.broadcasted_iota(jnp.int32, logits.shape, 1)
    lane = lane_i.astype(F32)
    big = float(LANES)
    neg = -jnp.inf
    gmask = lane_i < N_GROUPS
    lg = jnp.where(gmask, logits, neg)
    m = jnp.max(lg, axis=-1, keepdims=True)
    p_grp = 1.0 / jnp.sum(jnp.where(gmask, jnp.exp(lg - m), 0.0), axis=-1, keepdims=True)
    g_sel = jnp.min(jnp.where(gmask & (lg == m), lane, big), axis=-1, keepdims=True)
    lane_grp = lax.shift_right_arithmetic(lane_i - N_GROUPS, EXPERTS_PER_GROUP.bit_length() - 1)
    emask = (lane_i >= N_GROUPS) & (lane_i < N_GROUPS + N_EXPERTS) & (lane_grp == g_sel.astype(jnp.int32))
    le = jnp.where(emask, logits, neg)
    v1 = jnp.max(le, axis=-1, keepdims=True)
    i1 = jnp.min(jnp.where(emask & (le == v1), lane, big), axis=-1, keepdims=True)
    emask2 = emask & (lane != i1)
    le2 = jnp.where(emask2, logits, neg)
    v2 = jnp.max(le2, axis=-1, keepdims=True)
    i2 = jnp.min(jnp.where(emask2 & (le2 == v2), lane, big), axis=-1, keepdims=True)
    t = jnp.exp(v2 - v1)
    den = 1.0 + t
    gate1 = p_grp / den
    gate2 = p_grp * t / den
    rt_ref[...] = jnp.where(lane_i == 0, i1 - N_GROUPS,
                            jnp.where(lane_i == 1, i2 - N_GROUPS,
                                      jnp.where(lane_i == 2, gate1, jnp.where(lane_i == 3, gate2, 0.0))))


def _ln_router(mix, x, g, b, wr_hi, wr_lo, br, tm=256):
    m, d = x.shape
    tm = min(tm, m)
    rows = lambda w: pl.BlockSpec((tm, w), lambda i: (i, 0))
    full = lambda s: pl.BlockSpec(s, lambda i: (0, 0))
    return pl.pallas_call(
        _ln_router_kernel,
        grid=(m // tm,),
        in_specs=[rows(d), rows(d), full((1, d)), full((1, d)),
                  full((d, LANES)), full((d, LANES)), full((1, LANES))],
        out_specs=[rows(d), rows(d), rows(LANES)],
        out_shape=[jax.ShapeDtypeStruct((m, d), F32), jax.ShapeDtypeStruct((m, d), BF16),
                   jax.ShapeDtypeStruct((m, LANES), F32)],
        compiler_params=pltpu.CompilerParams(
            dimension_semantics=("parallel",), vmem_limit_bytes=VMEM_LIMIT),
        name="ln_router",
    )(mix, x, g.reshape(1, d), b.reshape(1, d), wr_hi, wr_lo, br)


def _moe_kernel(te_ref, nt_ref, src_ref, dst_ref, x_hbm, gate_ref, wg_ref, wu_ref, wd_ref,
                y_hbm, xbuf, ybuf, sem_in, sem_out, *, tm):
    i = pl.program_id(0)
    n_tiles = nt_ref[0]
    slot = lax.rem(i, 2)

    def gather(tile, sl):
        base = tile * tm

        def body(r, c):
            pltpu.make_async_copy(x_hbm.at[pl.ds(src_ref[base + r], 1)], xbuf.at[sl, pl.ds(r, 1)],
                                  sem_in.at[sl]).start()
            return c

        lax.fori_loop(0, tm, body, 0, unroll=8)

    def scatter_wait():
        pltpu.make_async_copy(ybuf, y_hbm.at[pl.ds(0, tm)], sem_out).wait()

    @pl.when(i == 0)
    def _():
        gather(0, 0)

    @pl.when(i + 1 < n_tiles)
    def _():
        gather(i + 1, 1 - slot)

    @pl.when(i < n_tiles)
    def _():
        pltpu.make_async_copy(x_hbm.at[pl.ds(0, tm)], xbuf.at[slot], sem_in.at[slot]).wait()
        xb = xbuf[slot].astype(BF16)
        hg = jnp.dot(xb, wg_ref[0], preferred_element_type=F32)
        hu = jnp.dot(xb, wu_ref[0], preferred_element_type=F32)
        hact = hg * _sigmoid(hg) * hu * gate_ref[...]
        y = jnp.dot(hact.astype(BF16), wd_ref[0], preferred_element_type=F32)

        @pl.when(i > 0)
        def _():
            scatter_wait()

        ybuf[...] = y
        base = i * tm

        def body(r, c):
            pltpu.make_async_copy(ybuf.at[pl.ds(r, 1)], y_hbm.at[pl.ds(dst_ref[base + r], 1)], sem_out).start()
            return c

        lax.fori_loop(0, tm, body, 0, unroll=8)

        @pl.when(i == n_tiles - 1)
        def _():
            scatter_wait()
            fill = pltpu.make_async_copy(ybuf, y_hbm.at[pl.ds(y_hbm.shape[0] - tm, tm)], sem_out)
            fill.start()
            fill.wait()


def _moe(x1, route, wg, wu, wd, tm):
    m, d = x1.shape
    n_e, _, f = wg.shape
    tm = min(tm, m)
    pairs = 2 * m
    nt_max = pairs // tm + n_e
    rows = nt_max * tm

    e_flat = jnp.concatenate([route[:, 0], route[:, 1]]).astype(jnp.int32)
    g_flat = jnp.concatenate([route[:, 2], route[:, 3]])
    cnt = jnp.sum(e_flat[:, None] == jnp.arange(n_e, dtype=jnp.int32)[None, :], axis=0, dtype=jnp.int32)
    cnt_pad = (cnt + tm - 1) // tm * tm
    end_pad = jnp.cumsum(cnt_pad)
    off_pad = end_pad - cnt_pad
    start = jnp.cumsum(cnt) - cnt
    order = jnp.argsort(e_flat, stable=True).astype(jnp.int32)
    e_sorted = e_flat[order]
    dest = off_pad[e_sorted] + jnp.arange(pairs, dtype=jnp.int32) - start[e_sorted]
    row_pair = jnp.full((rows,), -1, jnp.int32).at[dest].set(order)
    valid = row_pair >= 0
    src_tok = jnp.where(valid, row_pair % m, 0)
    dst_row = jnp.where(valid, row_pair, pairs + jnp.arange(rows, dtype=jnp.int32) % tm)
    row_gate = jnp.where(valid, g_flat[jnp.maximum(row_pair, 0)], 0.0).reshape(rows, 1)
    n_tiles = (end_pad[-1] // tm).astype(jnp.int32).reshape(1)
    tile_e = jnp.searchsorted(end_pad, jnp.arange(nt_max, dtype=jnp.int32) * tm, side='right')
    tile_e = jnp.minimum(tile_e, n_e - 1).astype(jnp.int32)

    grid_spec = pltpu.PrefetchScalarGridSpec(
        num_scalar_prefetch=4,
        grid=(nt_max,),
        in_specs=[pl.BlockSpec(memory_space=pl.ANY),
                  pl.BlockSpec((tm, 1), lambda i, te, nt, s, dd: (i, 0)),
                  pl.BlockSpec((1, d, f), lambda i, te, nt, s, dd: (te[i], 0, 0)),
                  pl.BlockSpec((1, d, f), lambda i, te, nt, s, dd: (te[i], 0, 0)),
                  pl.BlockSpec((1, f, d), lambda i, te, nt, s, dd: (te[i], 0, 0))],
        out_specs=pl.BlockSpec(memory_space=pl.ANY),
        scratch_shapes=[pltpu.VMEM((2, tm, d), F32), pltpu.VMEM((tm, d), F32),
                        pltpu.SemaphoreType.DMA((2,)), pltpu.SemaphoreType.DMA],
    )
    return pl.pallas_call(
        functools.partial(_moe_kernel, tm=tm),
        grid_spec=grid_spec,
        out_shape=jax.ShapeDtypeStruct((pairs + tm, d), F32),
        compiler_params=pltpu.CompilerParams(
            dimension_semantics=("arbitrary",), vmem_limit_bytes=VMEM_LIMIT),
        name="moe_experts",
    )(tile_e, n_tiles, src_tok, dst_row, x1, row_gate, wg, wu, wd)


def _ple_ln_kernel(x1_ref, x1b_ref, y0_ref, y1_ref, p_ref, wp_ref, gd_ref, gu_ref, g_ref, b_ref,
                   x2_ref, x2b_ref):
    d = lambda a, w: jnp.dot(a, w, preferred_element_type=F32)
    proj = d(p_ref[...].astype(BF16), wp_ref[...])
    gate = _sigmoid(d(d(x1b_ref[...], gd_ref[...]).astype(BF16), gu_ref[...]))
    y = DN_ALPHA * x1_ref[...] + (y0_ref[...] + y1_ref[...]) + proj * gate
    x2 = _layer_norm(y, g_ref[...], b_ref[...])
    x2_ref[...] = x2
    x2b_ref[...] = x2.astype(BF16)


def _ple_ln(x1, x1b, y, p, wp, gd, gu, g, b, tm=128):
    m, d = x1.shape
    pd = p.shape[1]
    tm = min(tm, m)
    nb = m // tm
    rows = lambda w: pl.BlockSpec((tm, w), lambda i: (i, 0))
    full = lambda s: pl.BlockSpec(s, lambda i: (0, 0))
    return pl.pallas_call(
        _ple_ln_kernel,
        grid=(nb,),
        in_specs=[rows(d), rows(d), rows(d), pl.BlockSpec((tm, d), lambda i: (i + nb, 0)), rows(pd),
                  full((pd, d)), full((d, pd)), full((pd, d)), full((1, d)), full((1, d))],
        out_specs=[rows(d), rows(d)],
        out_shape=[jax.ShapeDtypeStruct((m, d), F32), jax.ShapeDtypeStruct((m, d), BF16)],
        compiler_params=pltpu.CompilerParams(
            dimension_semantics=("parallel",), vmem_limit_bytes=VMEM_LIMIT),
        name="ple_ln",
    )(x1, x1b, y, y, p, wp, gd, gu, g.reshape(1, d), b.reshape(1, d))


def _pad_rows(w, rows, offset):
    out = jnp.zeros((rows, w.shape[1]), F32)
    return out.at[offset:offset + w.shape[0]].set(w.astype(F32)).astype(BF16)


def _layer(x, xb, p, w_in, conv_w, a_log, dt_bias, norm_w, mu, w0, w2, a0, a2, g2, k_k, k_a, r_k,
           lnx_g, lnx_b, w_out, ln1_g, ln1_b, w_grp, b_grp, w_rt, b_rt, wg, wu, wd,
           wp, gd, gu, ln2_g, ln2_b, *, batch, tb_gdn=512, tb_rwkv=256):
    m, d_model = x.shape
    t = m // batch
    d_mix = w_out.shape[0]
    d_delta = d_mix // 2
    d_rwkv = d_mix - d_delta
    nh = d_delta // GDN_HEAD_DIM
    n_ab = 2 * nh
    lora_w, lora_a = w2.shape[0], a2.shape[0]

    w_p = jnp.concatenate(
        [w_in[:, :4 * d_delta + n_ab].astype(BF16), jnp.zeros((d_model, AB_PAD - n_ab), BF16),
         w_in[:, 4 * d_delta + n_ab:].astype(BF16)], axis=1)
    col_rwkv = 4 * d_delta + AB_PAD

    h = _matmul(xb, w_p)
    h3 = h.reshape(batch, t, h.shape[1])
    o_gdn = _gdn(h3, conv_w, a_log, dt_bias, norm_w, d_delta, tb_gdn)
    o_rwkv = _rwkv(h3, col_rwkv, d_rwkv, mu, w0,
                   _pad_rows(w2, LORA_PAD, 0), a0, _pad_rows(a2, LORA_PAD, lora_w),
                   _pad_rows(g2, LORA_PAD, lora_w + lora_a),
                   k_k, k_a, r_k, lnx_g, lnx_b, tb_rwkv)
    mix_in = jnp.concatenate([o_gdn, o_rwkv], axis=-1).reshape(m, d_mix)
    mix = _matmul(mix_in, w_out.astype(BF16))

    w_router = jnp.concatenate(
        [w_grp, w_rt, jnp.zeros((d_model, LANES - N_GROUPS - N_EXPERTS), F32)], axis=1).astype(F32)
    wr_hi, wr_lo = _split_bf16(w_router)
    b_router = jnp.concatenate(
        [b_grp, b_rt, jnp.zeros((LANES - N_GROUPS - N_EXPERTS,), F32)]).astype(F32).reshape(1, LANES)
    x1, x1b, route = _ln_router(mix, x, ln1_g, ln1_b, wr_hi, wr_lo, b_router)

    y = _moe(x1, route, wg.astype(BF16), wu.astype(BF16), wd.astype(BF16), MOE_TILE)
    return _ple_ln(x1, x1b, y, p, wp.astype(BF16), gd.astype(BF16), gu.astype(BF16), ln2_g, ln2_b)


def kernel(x, p, w_in, gdn_conv_w, gdn_a_log, gdn_dt_bias, gdn_norm_w, rwkv_mu, rwkv_w0, rwkv_w2, rwkv_a0, rwkv_a2, rwkv_g2, rwkv_k_k, rwkv_k_a, rwkv_r_k, rwkv_lnx_g, rwkv_lnx_b, w_out, ln1_g, ln1_b, moe_w_grp, moe_b_grp, moe_w_rt, moe_b_rt, moe_w_gate, moe_w_up, moe_w_down, ple_w_proj, ple_w_gate_down, ple_w_gate_up, ln2_g, ln2_b):
    batch, t, d_model = x.shape
    m = batch * t
    xf = x.reshape(m, d_model).astype(F32)
    xb = xf.astype(BF16)
    for i in range(w_in.shape[0]):
        xf, xb = _layer(
            xf, xb, p[i].reshape(m, -1), w_in[i], gdn_conv_w[i], gdn_a_log[i], gdn_dt_bias[i],
            gdn_norm_w[i], rwkv_mu[i], rwkv_w0[i], rwkv_w2[i], rwkv_a0[i], rwkv_a2[i], rwkv_g2[i],
            rwkv_k_k[i], rwkv_k_a[i], rwkv_r_k[i], rwkv_lnx_g[i], rwkv_lnx_b[i], w_out[i],
            ln1_g[i], ln1_b[i], moe_w_grp[i], moe_b_grp[i], moe_w_rt[i], moe_b_rt[i],
            moe_w_gate[i], moe_w_up[i], moe_w_down[i], ple_w_proj[i], ple_w_gate_down[i],
            ple_w_gate_up[i], ln2_g[i], ln2_b[i], batch=batch)
    return xf.reshape(batch, t, d_model)
```

```python
import functools

import jax
import jax.numpy as jnp
from jax import lax
from jax.experimental import pallas as pl
from jax.experimental.pallas import tpu as pltpu

F32 = jnp.float32
BF16 = jnp.bfloat16

DEPTH = 4
DN_ALPHA = (2 * DEPTH) ** 0.25
CHUNK = 64
LANES = 128
GDN_HEAD_DIM = 128
GDN_CONV = 4
RWKV_HEAD_DIM = 64
RWKV_LNX_EPS = 64e-5
N_GROUPS = 4
EXPERTS_PER_GROUP = 8
N_EXPERTS = N_GROUPS * EXPERTS_PER_GROUP
LORA_PAD = 256
AB_PAD = 256
HALO = 8
MOE_TILE = 256
STREAMS = 2
VMEM_LIMIT = 48 * 1024 * 1024
MOE_VMEM_LIMIT = 58 * 1024 * 1024


def _sigmoid(x):
    return 1.0 / (1.0 + jnp.exp(-x))


def _softplus(x):
    return jnp.maximum(x, 0.0) + jnp.log(1.0 + jnp.exp(-jnp.abs(x)))


def _split_bf16(x):
    hi = x.astype(BF16)
    lo = (x - hi.astype(F32)).astype(BF16)
    return hi, lo


def _bmm(a, b):
    return jnp.einsum('nik,nkj->nij', a.astype(BF16), b.astype(BF16), preferred_element_type=F32)


def _bmm_nt(a, b):
    return jnp.einsum('nik,njk->nij', a.astype(BF16), b.astype(BF16), preferred_element_type=F32)


def _dot(a, b):
    return jnp.dot(a.astype(BF16), b.astype(BF16), preferred_element_type=F32)


def _dot_tn(a, b):
    return lax.dot_general(a.astype(BF16), b.astype(BF16), (((0,), (0,)), ((), ())),
                           preferred_element_type=F32)


def _neumann_inverse(m, steps):
    n = m.shape[-1]
    eye = (lax.broadcasted_iota(jnp.int32, (n, n), 0) == lax.broadcasted_iota(jnp.int32, (n, n), 1))
    p = m + eye.astype(F32)[None]
    mp = _bmm(m, m)
    for _ in range(steps - 1):
        both = _bmm(jnp.concatenate([p, mp], axis=1), mp)
        p = p + both[:, :n]
        mp = both[:, n:]
    return p + _bmm(p, mp)


def _chunk_cumsum(x, chunk):
    row = lax.broadcasted_iota(jnp.int32, x.shape, 0) & (chunk - 1)
    s = 1
    while s < chunk:
        x = x + jnp.where(row >= s, pltpu.roll(x, s, 0), 0.0)
        s *= 2
    return x


def _layer_norm(y, g, b):
    yc = y - jnp.mean(y, axis=-1, keepdims=True)
    var = jnp.mean(yc * yc, axis=-1, keepdims=True)
    return yc * lax.rsqrt(var + 1e-5) * g + b


def _lane_slab(x, j):
    return x[..., j * LANES:(j + 1) * LANES]


def _mm_kernel(x_ref, w_ref, o_ref):
    o_ref[...] = jnp.dot(x_ref[...], w_ref[0], preferred_element_type=F32)


def _matmul(x, w, layer, tm=1024, tn=512):
    m, k = x.shape
    n = w.shape[2]
    tm, tn = min(tm, m), min(tn, n)
    return pl.pallas_call(
        _mm_kernel,
        grid=(m // tm, n // tn),
        in_specs=[pl.BlockSpec((tm, k), lambda i, j: (i, 0)),
                  pl.BlockSpec((1, k, tn), lambda i, j: (layer, 0, j))],
        out_specs=pl.BlockSpec((tm, tn), lambda i, j: (i, j)),
        out_shape=jax.ShapeDtypeStruct((m, n), F32),
        compiler_params=pltpu.CompilerParams(
            dimension_semantics=("parallel", "parallel"), vmem_limit_bytes=VMEM_LIMIT),
        name="proj_matmul",
    )(x, w)


def _gdn_kernel(q_ref, k_ref, v_ref, z_ref, ab_ref, cwq_ref, cwk_ref, cwv_ref,
                alog_ref, dtb_ref, nw_ref, o_ref, ext_ref, s_ref, *, tb, n_heads, group):
    hg = pl.program_id(1)
    t_idx = pl.program_id(2)
    nc = tb // CHUNK
    width = group * LANES

    @pl.when(t_idx == 0)
    def _():
        ext_ref[:, 0:HALO, :] = jnp.zeros((3, HALO, width), F32)
        s_ref[...] = jnp.zeros_like(s_ref)

    def conv_silu(idx, x_ref, cw_ref):
        ext_ref[idx, HALO:HALO + tb, :] = x_ref[0]
        cw = cw_ref[...]
        acc = ext_ref[idx, HALO - 3:HALO - 3 + tb, :] * cw[0:1]
        for j in range(1, GDN_CONV):
            acc = acc + ext_ref[idx, HALO - 3 + j:HALO - 3 + j + tb, :] * cw[j:j + 1]
        ext_ref[idx, 0:HALO, :] = ext_ref[idx, tb:tb + HALO, :]
        return acc * _sigmoid(acc)

    q_all = conv_silu(0, q_ref, cwq_ref)
    k_all = conv_silu(1, k_ref, cwk_ref)
    v_all = conv_silu(2, v_ref, cwv_ref)
    ab = ab_ref[0]
    lane = lax.broadcasted_iota(jnp.int32, ab.shape, 1)
    ri = lax.broadcasted_iota(jnp.int32, (CHUNK, CHUNK), 0)
    ci = lax.broadcasted_iota(jnp.int32, (CHUNK, CHUNK), 1)
    causal = (ri >= ci)[None]
    strict = (ri > ci)[None]
    lane3 = lax.broadcasted_iota(jnp.int32, (nc, CHUNK, LANES), 2)
    c3 = lambda t: t.reshape(nc, CHUNK, t.shape[-1])

    streams = []
    for j in range(group):
        h = hg * group + j
        q = _lane_slab(q_all, j)
        k = _lane_slab(k_all, j)
        q = q * lax.rsqrt(jnp.sum(q * q, axis=-1, keepdims=True) + 1e-6) * (GDN_HEAD_DIM ** -0.5)
        k = k * lax.rsqrt(jnp.sum(k * k, axis=-1, keepdims=True) + 1e-6)
        a_col = jnp.sum(jnp.where(lane == h, ab, 0.0), axis=-1, keepdims=True)
        b_col = jnp.sum(jnp.where(lane == h + n_heads, ab, 0.0), axis=-1, keepdims=True)
        g = -jnp.exp(alog_ref[j]) * _softplus(a_col + dtb_ref[j])
        beta = jnp.broadcast_to(_sigmoid(b_col), g.shape)
        g = _chunk_cumsum(g, CHUNK)
        q3, k3, v3, g3, beta3 = c3(q), c3(k), c3(_lane_slab(v_all, j)), c3(g), c3(beta)
        g_last = g3[:, CHUNK - 1:CHUNK, :]

        p1 = g3.astype(BF16).astype(F32)
        p2 = (g3 - p1).astype(BF16).astype(F32)
        p3 = (g3 - p1 - p2).astype(BF16).astype(F32)
        pick = jnp.where((lane3 == 0) | (lane3 == 3), p1, jnp.where((lane3 == 1) | (lane3 == 4), p2, p3))
        col_op = jnp.where(lane3 < 3, pick, jnp.where(lane3 < 6, 1.0, 0.0))
        row_op = jnp.where(lane3 < 3, 1.0, jnp.where(lane3 < 6, -pick, 0.0))
        gdiff = _bmm_nt(col_op, row_op)
        decay = jnp.where(causal, jnp.exp(jnp.where(causal, gdiff, 0.0)), 0.0)

        kb3 = k3 * beta3
        kq_k = _bmm_nt(jnp.concatenate([kb3, q3], axis=1), k3)
        a_low = jnp.where(strict, kq_k[:, :CHUNK] * decay, 0.0)
        tinv = _neumann_inverse(-a_low, 5)
        eg = jnp.exp(g3)
        uw = _bmm(tinv, jnp.concatenate([v3 * beta3, kb3 * eg], axis=-1))
        streams.append(dict(
            u=uw[..., :GDN_HEAD_DIM], wq=jnp.concatenate([uw[..., GDN_HEAD_DIM:], q3 * eg], axis=1),
            attn=kq_k[:, CHUNK:] * decay, kd=k3 * jnp.exp(g_last - g3), gl=jnp.exp(g_last),
            s=s_ref[j], outs=[]))

    for c in range(nc):
        for st in streams:
            s = st['s']
            wq_s = _dot(st['wq'][c], s)
            v_new = st['u'][c] - wq_s[:CHUNK]
            st['outs'].append(wq_s[CHUNK:] + _dot(st['attn'][c], v_new))
            st['s'] = s * st['gl'][c] + _dot_tn(st['kd'][c], v_new)

    for j, st in enumerate(streams):
        s_ref[j] = st['s']
        o = jnp.concatenate(st['outs'], axis=0)
        o = o * lax.rsqrt(jnp.mean(o * o, axis=-1, keepdims=True) + 1e-6) * nw_ref[...]
        z = _lane_slab(z_ref[0], j)
        o_ref[0, :, j * LANES:(j + 1) * LANES] = (o * (z * _sigmoid(z))).astype(o_ref.dtype)


def _gdn(h3, conv_w, a_log, dt_bias, norm_w, d_delta, tb, group=STREAMS):
    b, t, _ = h3.shape
    nh = d_delta // GDN_HEAD_DIM
    tb = min(tb, t)
    group = min(group, nh)
    width = group * LANES
    hb = d_delta // width
    col = lambda off: pl.BlockSpec((1, tb, width), lambda bi, hi, ti, off=off: (bi, ti, off + hi))
    cw = lambda off: pl.BlockSpec((GDN_CONV, width), lambda bi, hi, ti, off=off: (0, off + hi))
    per_head = pl.BlockSpec((group, 1, LANES), lambda bi, hi, ti: (hi, 0, 0))
    alog_b = jnp.broadcast_to(a_log.astype(F32)[:, None, None], (nh, 1, LANES))
    dtb_b = jnp.broadcast_to(dt_bias.astype(F32)[:, None, None], (nh, 1, LANES))
    return pl.pallas_call(
        functools.partial(_gdn_kernel, tb=tb, n_heads=nh, group=group),
        grid=(b, nh // group, t // tb),
        in_specs=[col(0), col(hb), col(2 * hb), col(3 * hb),
                  pl.BlockSpec((1, tb, LANES), lambda bi, hi, ti: (bi, ti, 4 * d_delta // LANES)),
                  cw(0), cw(hb), cw(2 * hb), per_head, per_head,
                  pl.BlockSpec((1, LANES), lambda bi, hi, ti: (0, 0))],
        out_specs=pl.BlockSpec((1, tb, width), lambda bi, hi, ti: (bi, ti, hi)),
        out_shape=jax.ShapeDtypeStruct((b, t, d_delta), BF16),
        scratch_shapes=[pltpu.VMEM((3, HALO + tb, width), F32),
                        pltpu.VMEM((group, GDN_HEAD_DIM, GDN_HEAD_DIM), F32)],
        compiler_params=pltpu.CompilerParams(
            dimension_semantics=("parallel", "parallel", "arbitrary"), vmem_limit_bytes=VMEM_LIMIT),
        name="gdn_chunk",
    )(h3, h3, h3, h3, h3, conv_w, conv_w, conv_w, alog_b, dtb_b, norm_w.reshape(1, LANES))


def _rwkv_kernel(r_ref, k_ref, v_ref, xl_ref, mur_ref, muk_ref, muv_ref, mul_ref,
                 w0_ref, a0_ref, kk_ref, ka_ref, rk_ref, lg_ref, lb_ref,
                 w2_ref, a2_ref, g2_ref, o_ref, ext_ref, extl_ref, s_ref, *, tb, group):
    t_idx = pl.program_id(2)
    nc = tb // CHUNK
    c2 = 2 * CHUNK
    width = group * LANES

    @pl.when(t_idx == 0)
    def _():
        ext_ref[:, 0:HALO, :] = jnp.zeros((3, HALO, width), F32)
        extl_ref[0:HALO, :] = jnp.zeros((HALO, LORA_PAD), F32)
        s_ref[...] = jnp.zeros_like(s_ref)

    def shift(ext, x, mu):
        ext[HALO:HALO + tb, :] = x
        prev = ext[HALO - 1:HALO - 1 + tb, :]
        ext[0:HALO, :] = ext[tb:tb + HALO, :]
        return x + mu * (prev - x)

    r = shift(ext_ref.at[0], r_ref[0], mur_ref[...])
    k = shift(ext_ref.at[1], k_ref[0], muk_ref[...])
    v = shift(ext_ref.at[2], v_ref[0], muv_ref[...])
    xl = shift(extl_ref, xl_ref[0], mul_ref[...])

    w_log = -_softplus(-(w0_ref[...] + _dot(jnp.tanh(xl), w2_ref[...]))) - 0.5
    ld = -jnp.exp(w_log)
    a = _sigmoid(a0_ref[...] + _dot(xl, a2_ref[...]))
    gate = _dot(_sigmoid(xl), g2_ref[...])

    head0 = lax.broadcasted_iota(jnp.int32, (tb, LANES), 1) < RWKV_HEAD_DIM

    def head_sum(x):
        parts = []
        for j in range(group):
            xj = _lane_slab(x, j)
            s0 = jnp.sum(jnp.where(head0, xj, 0.0), axis=-1, keepdims=True)
            s1 = jnp.sum(jnp.where(head0, 0.0, xj), axis=-1, keepdims=True)
            parts.append(jnp.where(head0, s0, s1))
        return jnp.concatenate(parts, axis=-1) if group > 1 else parts[0]

    kk = k * kk_ref[...]
    kk = kk * lax.rsqrt(head_sum(kk * kk) + 1e-6)
    k2 = k * (1.0 + (a - 1.0) * ka_ref[...])
    bb = kk * a

    lc = _chunk_cumsum(ld, CHUNK)
    c3 = lambda t: t.reshape(nc, CHUNK, t.shape[-1])
    lc3 = c3(lc)
    l_last = lc3[:, CHUNK - 1:CHUNK, :]
    e_neg = jnp.exp(-lc)
    a_t = c3(-kk * jnp.exp(lc - ld))
    b_t = c3(bb * e_neg)
    k_t = c3(k2 * e_neg)
    r_t = c3(r * jnp.exp(lc))
    e_tail = jnp.exp(l_last - lc3)
    b_h = c3(bb) * e_tail
    k_h = c3(k2) * e_tail
    w_c = jnp.exp(l_last)
    v3 = c3(v)

    head0_3 = lax.broadcasted_iota(jnp.int32, (nc, CHUNK, LANES), 2) < RWKV_HEAD_DIM
    stack = lambda x: jnp.concatenate([jnp.where(head0_3, x, 0.0), jnp.where(head0_3, 0.0, x)], axis=1)
    dup = lambda x: jnp.concatenate([x, x], axis=1)
    ri = lax.broadcasted_iota(jnp.int32, (c2, c2), 0)
    ci = lax.broadcasted_iota(jnp.int32, (c2, c2), 1)
    same = (ri >= CHUNK) == (ci >= CHUNK)
    strict = (same & (ri > ci))[None]
    incl = (same & (ri >= ci))[None]
    h0c = lax.broadcasted_iota(jnp.int32, (CHUNK, LANES), 1) < RWKV_HEAD_DIM
    bd = ((lax.broadcasted_iota(jnp.int32, (LANES, LANES), 0) >= RWKV_HEAD_DIM)
          == (lax.broadcasted_iota(jnp.int32, (LANES, LANES), 1) >= RWKV_HEAD_DIM))

    streams = []
    for j in range(group):
        sl = lambda x: _lane_slab(x, j)
        a_s, r_s = stack(sl(a_t)), stack(sl(r_t))
        vv2 = dup(sl(v3))
        ar_bk = _bmm_nt(jnp.concatenate([a_s, r_s], axis=1),
                        jnp.concatenate([dup(sl(b_t)), dup(sl(k_t))], axis=1))
        m_ab = jnp.where(strict, ar_bk[:, :c2, :c2], 0.0)
        m_ak = jnp.where(strict, ar_bk[:, :c2, c2:], 0.0)
        m_rb = jnp.where(incl, ar_bk[:, c2:, :c2], 0.0)
        m_rk = jnp.where(incl, ar_bk[:, c2:, c2:], 0.0)
        tinv = _neumann_inverse(m_ab, 5)
        mv = _bmm(jnp.concatenate([m_ak, m_rk], axis=1), vv2)
        pt = _bmm(tinv, jnp.concatenate([mv[:, :c2], a_s], axis=-1))
        w_col = [jnp.broadcast_to(sl(w_c)[c], (LANES, LANES)).T for c in range(nc)]
        streams.append(dict(
            p=pt[..., :LANES], tr=jnp.concatenate([pt[..., LANES:], sl(r_t)], axis=1), yv=mv[:, c2:],
            m_rb=m_rb, v=sl(v3), bk=jnp.concatenate([sl(b_h), sl(k_h)], axis=1), w_col=w_col,
            s=s_ref[j], outs=[]))

    for c in range(nc):
        for st in streams:
            s = st['s']
            tr_s = _dot(st['tr'][c], s)
            u_s = st['p'][c] + tr_s[:c2]
            u = jnp.where(h0c, u_s[:CHUNK], u_s[CHUNK:])
            y_s = _dot(st['m_rb'][c], jnp.concatenate([u, u], axis=0)) + st['yv'][c]
            st['outs'].append(tr_s[c2:] + jnp.where(h0c, y_s[:CHUNK], y_s[CHUNK:]))
            upd = _dot_tn(st['bk'][c], jnp.concatenate([u, st['v'][c]], axis=0))
            st['s'] = s * st['w_col'][c] + jnp.where(bd, upd, 0.0)

    for j, st in enumerate(streams):
        s_ref[j] = st['s']
    y = jnp.concatenate([jnp.concatenate(st['outs'], axis=0) for st in streams], axis=-1)

    inv_n = 1.0 / RWKV_HEAD_DIM
    yc = y - head_sum(y) * inv_n
    y = yc * lax.rsqrt(head_sum(yc * yc) * inv_n + RWKV_LNX_EPS)
    y = y * lg_ref[...] + lb_ref[...]
    y = y + head_sum(r * k2 * rk_ref[...]) * v
    o_ref[0] = (y * gate).astype(o_ref.dtype)


def _rwkv(h3, col0, d_rwkv, mu, w0, w2p, a0, a2p, g2p, k_k, k_a, r_k, lnx_g, lnx_b, tb, group=STREAMS):
    b, t, _ = h3.shape
    tb = min(tb, t)
    group = min(group, d_rwkv // LANES)
    width = group * LANES
    nblk = d_rwkv // width
    cb = col0 // width
    lora_blk = (col0 + 3 * d_rwkv) // LORA_PAD
    col = lambda off: pl.BlockSpec((1, tb, width), lambda bi, hi, ti, off=off: (bi, ti, cb + off + hi))
    vec = lambda off: pl.BlockSpec((1, width), lambda bi, hi, ti, off=off: (0, off + hi))
    lw = pl.BlockSpec((LORA_PAD, width), lambda bi, hi, ti: (0, hi))
    row = lambda x: x.astype(F32).reshape(1, -1)
    mu2 = row(mu)
    return pl.pallas_call(
        functools.partial(_rwkv_kernel, tb=tb, group=group),
        grid=(b, nblk, t // tb),
        in_specs=[col(0), col(nblk), col(2 * nblk),
                  pl.BlockSpec((1, tb, LORA_PAD), lambda bi, hi, ti: (bi, ti, lora_blk)),
                  vec(0), vec(nblk), vec(2 * nblk),
                  pl.BlockSpec((1, LORA_PAD), lambda bi, hi, ti: (0, 3 * d_rwkv // LORA_PAD)),
                  vec(0), vec(0), vec(0), vec(0), vec(0), vec(0), vec(0),
                  lw, lw, lw],
        out_specs=pl.BlockSpec((1, tb, width), lambda bi, hi, ti: (bi, ti, hi)),
        out_shape=jax.ShapeDtypeStruct((b, t, d_rwkv), BF16),
        scratch_shapes=[pltpu.VMEM((3, HALO + tb, width), F32),
                        pltpu.VMEM((HALO + tb, LORA_PAD), F32),
                        pltpu.VMEM((group, LANES, LANES), F32)],
        compiler_params=pltpu.CompilerParams(
            dimension_semantics=("parallel", "parallel", "arbitrary"), vmem_limit_bytes=VMEM_LIMIT),
        name="rwkv_chunk",
    )(h3, h3, h3, h3, mu2, mu2, mu2, mu2, row(w0), row(a0), row(k_k), row(k_a), row(r_k),
      row(lnx_g), row(lnx_b), w2p, a2p, g2p)


def _ln_router_kernel(mix_ref, x_ref, g_ref, b_ref, wh_ref, wl_ref, br_ref, x1_ref, x1b_ref, rt_ref):
    x1 = _layer_norm(DN_ALPHA * x_ref[...] + mix_ref[...], g_ref[...], b_ref[...])
    x1_ref[...] = x1
    x1b_ref[...] = x1.astype(BF16)
    hi, lo = _split_bf16(x1)
    d = lambda a, w: jnp.dot(a, w, preferred_element_type=F32)
    logits = d(hi, wh_ref[...]) + d(lo, wh_ref[...]) + d(hi, wl_ref[...]) + br_ref[...]

    lane_i = lax.broadcasted_iota(jnp.int32, logits.shape, 1)
    lane = lane_i.astype(F32)
    big = float(LANES)
    neg = -jnp.inf
    gmask = lane_i < N_GROUPS
    lg = jnp.where(gmask, logits, neg)
    m = jnp.max(lg, axis=-1, keepdims=True)
    p_grp = 1.0 / jnp.sum(jnp.where(gmask, jnp.exp(lg - m), 0.0), axis=-1, keepdims=True)
    g_sel = jnp.min(jnp.where(gmask & (lg == m), lane, big), axis=-1, keepdims=True)
    lane_grp = lax.shift_right_arithmetic(lane_i - N_GROUPS, EXPERTS_PER_GROUP.bit_length() - 1)
    emask = (lane_i >= N_GROUPS) & (lane_i < N_GROUPS + N_EXPERTS) & (lane_grp == g_sel.astype(jnp.int32))
    le = jnp.where(emask, logits, neg)
    v1 = jnp.max(le, axis=-1, keepdims=True)
    i1 = jnp.min(jnp.where(emask & (le == v1), lane, big), axis=-1, keepdims=True)
    emask2 = emask & (lane != i1)
    le2 = jnp.where(emask2, logits, neg)
    v2 = jnp.max(le2, axis=-1, keepdims=True)
    i2 = jnp.min(jnp.where(emask2 & (le2 == v2), lane, big), axis=-1, keepdims=True)
    t = jnp.exp(v2 - v1)
    den = 1.0 + t
    gate1 = p_grp / den
    gate2 = p_grp * t / den
    rt_ref[...] = jnp.where(lane_i == 0, i1 - N_GROUPS,
                            jnp.where(lane_i == 1, i2 - N_GROUPS,
                                      jnp.where(lane_i == 2, gate1, jnp.where(lane_i == 3, gate2, 0.0))))


def _ln_router(mix, x, g, b, wr_hi, wr_lo, br, tm=256):
    m, d = x.shape
    tm = min(tm, m)
    rows = lambda w: pl.BlockSpec((tm, w), lambda i: (i, 0))
    full = lambda s: pl.BlockSpec(s, lambda i: (0, 0))
    return pl.pallas_call(
        _ln_router_kernel,
        grid=(m // tm,),
        in_specs=[rows(d), rows(d), full((1, d)), full((1, d)),
                  full((d, LANES)), full((d, LANES)), full((1, LANES))],
        out_specs=[rows(d), rows(d), rows(LANES)],
        out_shape=[jax.ShapeDtypeStruct((m, d), F32), jax.ShapeDtypeStruct((m, d), BF16),
                   jax.ShapeDtypeStruct((m, LANES), F32)],
        compiler_params=pltpu.CompilerParams(
            dimension_semantics=("parallel",), vmem_limit_bytes=VMEM_LIMIT),
        name="ln_router",
    )(mix, x, g.reshape(1, d), b.reshape(1, d), wr_hi, wr_lo, br)


def _moe_kernel(te_ref, nt_ref, ts_ref, tl_ref, order_ref, x_hbm, wg_ref, wu_ref, wd_ref,
                y_hbm, xbuf, ybuf, sem_in, sem_out, *, tm):
    i = pl.program_id(0)
    n_tiles = nt_ref[0]
    slot = lax.rem(i, 2)
    n_tok = x_hbm.shape[0]
    n_pairs = y_hbm.shape[0] - tm

    def pairs_of(tile):
        start, length = ts_ref[tile], tl_ref[tile]
        for r in range(tm):
            valid = r < length
            yield r, valid, order_ref[jnp.where(valid, start + r, 0)]

    def gather(tile, sl):
        for r, valid, p in pairs_of(tile):
            tok = jnp.where(valid, jnp.where(p >= n_tok, p - n_tok, p), 0)
            pltpu.make_async_copy(x_hbm.at[pl.ds(tok, 1)], xbuf.at[sl, pl.ds(r, 1)], sem_in.at[sl]).start()

    def scatter_wait():
        pltpu.make_async_copy(ybuf, y_hbm.at[pl.ds(0, tm)], sem_out).wait()

    @pl.when(i == 0)
    def _():
        gather(0, 0)

    @pl.when(i + 1 < n_tiles)
    def _():
        gather(i + 1, 1 - slot)

    @pl.when(i < n_tiles)
    def _():
        pltpu.make_async_copy(x_hbm.at[pl.ds(0, tm)], xbuf.at[slot], sem_in.at[slot]).wait()
        xb = xbuf[slot].astype(BF16)
        hg = jnp.dot(xb, wg_ref[0, 0].astype(BF16), preferred_element_type=F32)
        hu = jnp.dot(xb, wu_ref[0, 0].astype(BF16), preferred_element_type=F32)
        hact = hg * _sigmoid(hg) * hu
        y = jnp.dot(hact.astype(BF16), wd_ref[0, 0].astype(BF16), preferred_element_type=F32)

        @pl.when(i > 0)
        def _():
            scatter_wait()

        ybuf[...] = y

        for r, valid, p in pairs_of(i):
            dst = jnp.where(valid, p, n_pairs + r)
            pltpu.make_async_copy(ybuf.at[pl.ds(r, 1)], y_hbm.at[pl.ds(dst, 1)], sem_out).start()

        @pl.when(i == n_tiles - 1)
        def _():
            scatter_wait()
            fill = pltpu.make_async_copy(ybuf, y_hbm.at[pl.ds(y_hbm.shape[0] - tm, tm)], sem_out)
            fill.start()
            fill.wait()


def _moe(x1, route, wg, wu, wd, layer, tm):
    m, d = x1.shape
    n_e, f = wg.shape[1], wg.shape[3]
    tm = min(tm, m)
    pairs = 2 * m
    nt_max = pairs // tm + n_e

    e_flat = jnp.concatenate([route[:, 0], route[:, 1]]).astype(jnp.int32)
    order = jnp.argsort(e_flat, stable=True).astype(jnp.int32)
    cnt = jnp.sum(e_flat[:, None] == jnp.arange(n_e, dtype=jnp.int32)[None, :], axis=0, dtype=jnp.int32)
    e_tiles = (cnt + tm - 1) // tm
    tile_end = jnp.cumsum(e_tiles)
    n_tiles = tile_end[-1:].astype(jnp.int32)
    tile_idx = jnp.arange(nt_max, dtype=jnp.int32)
    tile_e = jnp.minimum(jnp.searchsorted(tile_end, tile_idx, side='right'), n_e - 1).astype(jnp.int32)
    k_in_e = tile_idx - (tile_end - e_tiles)[tile_e]
    tile_start = ((jnp.cumsum(cnt) - cnt)[tile_e] + k_in_e * tm).astype(jnp.int32)
    tile_len = jnp.clip(cnt[tile_e] - k_in_e * tm, 0, tm).astype(jnp.int32)

    w_in_spec = pl.BlockSpec((1, 1, d, f), lambda i, te, nt, ts, tl, o: (layer, te[i], 0, 0))
    grid_spec = pltpu.PrefetchScalarGridSpec(
        num_scalar_prefetch=5,
        grid=(nt_max,),
        in_specs=[pl.BlockSpec(memory_space=pl.ANY), w_in_spec, w_in_spec,
                  pl.BlockSpec((1, 1, f, d), lambda i, te, nt, ts, tl, o: (layer, te[i], 0, 0))],
        out_specs=pl.BlockSpec(memory_space=pl.ANY),
        scratch_shapes=[pltpu.VMEM((2, tm, d), F32), pltpu.VMEM((tm, d), F32),
                        pltpu.SemaphoreType.DMA((2,)), pltpu.SemaphoreType.DMA],
    )
    return pl.pallas_call(
        functools.partial(_moe_kernel, tm=tm),
        grid_spec=grid_spec,
        out_shape=jax.ShapeDtypeStruct((pairs + tm, d), F32),
        compiler_params=pltpu.CompilerParams(
            dimension_semantics=("arbitrary",), vmem_limit_bytes=MOE_VMEM_LIMIT),
        name="moe_experts",
    )(tile_e, n_tiles, tile_start, tile_len, order, x1, wg, wu, wd)


def _ple_ln_kernel(x1_ref, x1b_ref, y0_ref, y1_ref, rt_ref, p_ref, wp_ref, gd_ref, gu_ref, g_ref, b_ref,
                   x2_ref, x2b_ref):
    d = lambda a, w: jnp.dot(a, w, preferred_element_type=F32)
    proj = d(p_ref[...].astype(BF16), wp_ref[0])
    gate = _sigmoid(d(d(x1b_ref[...], gd_ref[0]).astype(BF16), gu_ref[0]))
    rt = rt_ref[...]
    ffn = rt[:, 2:3] * y0_ref[...] + rt[:, 3:4] * y1_ref[...]
    y = DN_ALPHA * x1_ref[...] + ffn + proj * gate
    x2 = _layer_norm(y, g_ref[...], b_ref[...])
    x2_ref[...] = x2
    x2b_ref[...] = x2.astype(BF16)


def _ple_ln(x1, x1b, y, route, p, wp, gd, gu, g, b, layer, tm=128):
    m, d = x1.shape
    pd = p.shape[1]
    tm = min(tm, m)
    nb = m // tm
    rows = lambda w: pl.BlockSpec((tm, w), lambda i: (i, 0))
    full = lambda s: pl.BlockSpec(s, lambda i: (0, 0))
    stack = lambda r, c: pl.BlockSpec((1, r, c), lambda i: (layer, 0, 0))
    return pl.pallas_call(
        _ple_ln_kernel,
        grid=(nb,),
        in_specs=[rows(d), rows(d), rows(d), pl.BlockSpec((tm, d), lambda i: (i + nb, 0)), rows(LANES),
                  rows(pd), stack(pd, d), stack(d, pd), stack(pd, d), full((1, d)), full((1, d))],
        out_specs=[rows(d), rows(d)],
        out_shape=[jax.ShapeDtypeStruct((m, d), F32), jax.ShapeDtypeStruct((m, d), BF16)],
        compiler_params=pltpu.CompilerParams(
            dimension_semantics=("parallel",), vmem_limit_bytes=VMEM_LIMIT),
        name="ple_ln",
    )(x1, x1b, y, y, route, p, wp, gd, gu, g.reshape(1, d), b.reshape(1, d))


def _pad_rows(w, rows, offset):
    out = jnp.zeros((rows, w.shape[1]), F32)
    return out.at[offset:offset + w.shape[0]].set(w.astype(F32)).astype(BF16)


def _layer(layer, x, xb, p, w_p, conv_w, a_log, dt_bias, norm_w, mu, w0, w2, a0, a2, g2, k_k, k_a, r_k,
           lnx_g, lnx_b, w_out, ln1_g, ln1_b, w_grp, b_grp, w_rt, b_rt, wg, wu, wd,
           wp, gd, gu, ln2_g, ln2_b, *, batch, tb_gdn=512, tb_rwkv=512):
    m, d_model = x.shape
    t = m // batch
    d_mix = w_out.shape[1]
    d_delta = d_mix // 2
    d_rwkv = d_mix - d_delta
    lora_w, lora_a = w2.shape[0], a2.shape[0]
    col_rwkv = 4 * d_delta + AB_PAD

    h = _matmul(xb, w_p, layer)
    h3 = h.reshape(batch, t, h.shape[1])
    o_gdn = _gdn(h3, conv_w, a_log, dt_bias, norm_w, d_delta, tb_gdn)
    o_rwkv = _rwkv(h3, col_rwkv, d_rwkv, mu, w0,
                   _pad_rows(w2, LORA_PAD, 0), a0, _pad_rows(a2, LORA_PAD, lora_w),
                   _pad_rows(g2, LORA_PAD, lora_w + lora_a),
                   k_k, k_a, r_k, lnx_g, lnx_b, tb_rwkv)
    mix_in = jnp.concatenate([o_gdn, o_rwkv], axis=-1).reshape(m, d_mix)
    mix = _matmul(mix_in, w_out, layer)

    w_router = jnp.concatenate(
        [w_grp, w_rt, jnp.zeros((d_model, LANES - N_GROUPS - N_EXPERTS), F32)], axis=1).astype(F32)
    wr_hi, wr_lo = _split_bf16(w_router)
    b_router = jnp.concatenate(
        [b_grp, b_rt, jnp.zeros((LANES - N_GROUPS - N_EXPERTS,), F32)]).astype(F32).reshape(1, LANES)
    x1, x1b, route = _ln_router(mix, x, ln1_g, ln1_b, wr_hi, wr_lo, b_router)

    y = _moe(x1, route, wg, wu, wd, layer, MOE_TILE)
    return _ple_ln(x1, x1b, y, route, p, wp, gd, gu, ln2_g, ln2_b, layer)


def kernel(x, p, w_in, gdn_conv_w, gdn_a_log, gdn_dt_bias, gdn_norm_w, rwkv_mu, rwkv_w0, rwkv_w2, rwkv_a0, rwkv_a2, rwkv_g2, rwkv_k_k, rwkv_k_a, rwkv_r_k, rwkv_lnx_g, rwkv_lnx_b, w_out, ln1_g, ln1_b, moe_w_grp, moe_b_grp, moe_w_rt, moe_b_rt, moe_w_gate, moe_w_up, moe_w_down, ple_w_proj, ple_w_gate_down, ple_w_gate_up, ln2_g, ln2_b):
    batch, t, d_model = x.shape
    m = batch * t
    xf = x.reshape(m, d_model).astype(F32)
    xb = xf.astype(BF16)

    n_layers = w_in.shape[0]
    d_delta = w_out.shape[1] // 2
    n_gdn = 4 * d_delta + 2 * (d_delta // GDN_HEAD_DIM)
    w_p = jnp.concatenate(
        [w_in[:, :, :n_gdn].astype(BF16), jnp.zeros((n_layers, d_model, 4 * d_delta + AB_PAD - n_gdn), BF16),
         w_in[:, :, n_gdn:].astype(BF16)], axis=2)
    w_out_b = w_out.astype(BF16)
    wp_b, gd_b, gu_b = (w.astype(BF16) for w in (ple_w_proj, ple_w_gate_down, ple_w_gate_up))

    for i in range(n_layers):
        xf, xb = _layer(
            i, xf, xb, p[i].reshape(m, -1), w_p, gdn_conv_w[i], gdn_a_log[i], gdn_dt_bias[i],
            gdn_norm_w[i], rwkv_mu[i], rwkv_w0[i], rwkv_w2[i], rwkv_a0[i], rwkv_a2[i], rwkv_g2[i],
            rwkv_k_k[i], rwkv_k_a[i], rwkv_r_k[i], rwkv_lnx_g[i], rwkv_lnx_b[i], w_out_b,
            ln1_g[i], ln1_b[i], moe_w_grp[i], moe_b_grp[i], moe_w_rt[i], moe_b_rt[i],
            moe_w_gate, moe_w_up, moe_w_down, wp_b, gd_b, gu_b, ln2_g[i], ln2_b[i], batch=batch)
    return xf.reshape(batch, t, d_model)
```

```python
import functools

import jax
import jax.numpy as jnp
from jax import lax
from jax.experimental import pallas as pl
from jax.experimental.pallas import tpu as pltpu

F32 = jnp.float32
BF16 = jnp.bfloat16

DEPTH = 4
DN_ALPHA = (2 * DEPTH) ** 0.25
CHUNK = 64
LANES = 128
GDN_HEAD_DIM = 128
GDN_CONV = 4
RWKV_HEAD_DIM = 64
RWKV_LNX_EPS = 64e-5
N_GROUPS = 4
EXPERTS_PER_GROUP = 8
N_EXPERTS = N_GROUPS * EXPERTS_PER_GROUP
LORA_PAD = 256
AB_PAD = 256
HALO = 8
MOE_TILE = 256
PROJ_TILE = 512
STREAMS = 2
VMEM_LIMIT = 48 * 1024 * 1024
MOE_VMEM_LIMIT = 58 * 1024 * 1024


def _sigmoid(x):
    return 1.0 / (1.0 + jnp.exp(-x))


def _softplus(x):
    return jnp.maximum(x, 0.0) + jnp.log(1.0 + jnp.exp(-jnp.abs(x)))


def _split_bf16(x):
    hi = x.astype(BF16)
    lo = (x - hi.astype(F32)).astype(BF16)
    return hi, lo


def _bmm(a, b):
    return jnp.einsum('nik,nkj->nij', a.astype(BF16), b.astype(BF16), preferred_element_type=F32)


def _bmm_nt(a, b):
    return jnp.einsum('nik,njk->nij', a.astype(BF16), b.astype(BF16), preferred_element_type=F32)


def _dot(a, b):
    return jnp.dot(a.astype(BF16), b.astype(BF16), preferred_element_type=F32)


def _dot_tn(a, b):
    return lax.dot_general(a.astype(BF16), b.astype(BF16), (((0,), (0,)), ((), ())),
                           preferred_element_type=F32)


def _neumann_inverse(m, steps):
    n = m.shape[-1]
    eye = (lax.broadcasted_iota(jnp.int32, (n, n), 0) == lax.broadcasted_iota(jnp.int32, (n, n), 1))
    p = m + eye.astype(F32)[None]
    mp = _bmm(m, m)
    for _ in range(steps - 1):
        both = _bmm(jnp.concatenate([p, mp], axis=1), mp)
        p = p + both[:, :n]
        mp = both[:, n:]
    return p + _bmm(p, mp)


def _chunk_cumsum(x, chunk):
    row = lax.broadcasted_iota(jnp.int32, x.shape, 0) & (chunk - 1)
    s = 1
    while s < chunk:
        x = x + jnp.where(row >= s, pltpu.roll(x, s, 0), 0.0)
        s *= 2
    return x


def _layer_norm(y, g, b):
    yc = y - jnp.mean(y, axis=-1, keepdims=True)
    var = jnp.mean(yc * yc, axis=-1, keepdims=True)
    return yc * lax.rsqrt(var + 1e-5) * g + b


def _lane_slab(x, j):
    return x[..., j * LANES:(j + 1) * LANES]


def _mm_kernel(x_ref, w_ref, o_ref):
    o_ref[...] = jnp.dot(x_ref[...], w_ref[0], preferred_element_type=F32)


def _matmul(x, w, layer, tm=1024, tn=512):
    m, k = x.shape
    n = w.shape[2]
    tm, tn = min(tm, m), min(tn, n)
    return pl.pallas_call(
        _mm_kernel,
        grid=(m // tm, n // tn),
        in_specs=[pl.BlockSpec((tm, k), lambda i, j: (i, 0)),
                  pl.BlockSpec((1, k, tn), lambda i, j: (layer, 0, j))],
        out_specs=pl.BlockSpec((tm, tn), lambda i, j: (i, j)),
        out_shape=jax.ShapeDtypeStruct((m, n), F32),
        compiler_params=pltpu.CompilerParams(
            dimension_semantics=("parallel", "parallel"), vmem_limit_bytes=VMEM_LIMIT),
        name="proj_matmul",
    )(x, w)


def _relayout_kernel(w_ref, o_ref, prev_ref, *, first_shifted, n_ab, shift):
    j = pl.program_id(2)
    keep = PROJ_TILE - shift

    @pl.when(j < first_shifted)
    def _():
        o_ref[0] = w_ref[0].astype(BF16)

    @pl.when(j == first_shifted)
    def _():
        w = w_ref[0]
        o_ref[0] = jnp.concatenate(
            [w[:, :n_ab], jnp.zeros((w.shape[0], shift), F32), w[:, n_ab:keep]], axis=1).astype(BF16)
        prev_ref[...] = w

    @pl.when(j > first_shifted)
    def _():
        w = w_ref[0]
        o_ref[0] = jnp.concatenate([prev_ref[:, keep:], w[:, :keep]], axis=1).astype(BF16)
        prev_ref[...] = w


def _relayout_w_in(w_in, d_delta, tr=1024):
    n_layers, d_model, n_src = w_in.shape
    n_ab = 2 * (d_delta // GDN_HEAD_DIM)
    shift = AB_PAD - n_ab
    n_out = n_src + shift
    tr = min(tr, d_model)
    return pl.pallas_call(
        functools.partial(_relayout_kernel, first_shifted=4 * d_delta // PROJ_TILE, n_ab=n_ab, shift=shift),
        grid=(n_layers, d_model // tr, n_out // PROJ_TILE),
        in_specs=[pl.BlockSpec((1, tr, PROJ_TILE), lambda l, r, j: (l, r, j))],
        out_specs=pl.BlockSpec((1, tr, PROJ_TILE), lambda l, r, j: (l, r, j)),
        out_shape=jax.ShapeDtypeStruct((n_layers, d_model, n_out), BF16),
        scratch_shapes=[pltpu.VMEM((tr, PROJ_TILE), F32)],
        compiler_params=pltpu.CompilerParams(
            dimension_semantics=("parallel", "parallel", "arbitrary"), vmem_limit_bytes=VMEM_LIMIT),
        name="relayout_w_in",
    )(w_in)


def _gdn_kernel(q_ref, k_ref, v_ref, z_ref, ab_ref, cwq_ref, cwk_ref, cwv_ref,
                alog_ref, dtb_ref, nw_ref, o_ref, ext_ref, s_ref, *, tb, n_heads, group):
    hg = pl.program_id(1)
    t_idx = pl.program_id(2)
    nc = tb // CHUNK
    width = group * LANES

    @pl.when(t_idx == 0)
    def _():
        ext_ref[:, 0:HALO, :] = jnp.zeros((3, HALO, width), F32)
        s_ref[...] = jnp.zeros_like(s_ref)

    def conv_silu(idx, x_ref, cw_ref):
        ext_ref[idx, HALO:HALO + tb, :] = x_ref[0]
        cw = cw_ref[...]
        acc = ext_ref[idx, HALO - 3:HALO - 3 + tb, :] * cw[0:1]
        for j in range(1, GDN_CONV):
            acc = acc + ext_ref[idx, HALO - 3 + j:HALO - 3 + j + tb, :] * cw[j:j + 1]
        ext_ref[idx, 0:HALO, :] = ext_ref[idx, tb:tb + HALO, :]
        return acc * _sigmoid(acc)

    q_all = conv_silu(0, q_ref, cwq_ref)
    k_all = conv_silu(1, k_ref, cwk_ref)
    v_all = conv_silu(2, v_ref, cwv_ref)
    ab = ab_ref[0]
    lane = lax.broadcasted_iota(jnp.int32, ab.shape, 1)
    ri = lax.broadcasted_iota(jnp.int32, (CHUNK, CHUNK), 0)
    ci = lax.broadcasted_iota(jnp.int32, (CHUNK, CHUNK), 1)
    causal = (ri >= ci)[None]
    strict = (ri > ci)[None]
    lane3 = lax.broadcasted_iota(jnp.int32, (nc, CHUNK, LANES), 2)
    c3 = lambda t: t.reshape(nc, CHUNK, t.shape[-1])

    streams = []
    for j in range(group):
        h = hg * group + j
        q = _lane_slab(q_all, j)
        k = _lane_slab(k_all, j)
        q = q * lax.rsqrt(jnp.sum(q * q, axis=-1, keepdims=True) + 1e-6) * (GDN_HEAD_DIM ** -0.5)
        k = k * lax.rsqrt(jnp.sum(k * k, axis=-1, keepdims=True) + 1e-6)
        a_col = jnp.sum(jnp.where(lane == h, ab, 0.0), axis=-1, keepdims=True)
        b_col = jnp.sum(jnp.where(lane == h + n_heads, ab, 0.0), axis=-1, keepdims=True)
        g = -jnp.exp(alog_ref[j]) * _softplus(a_col + dtb_ref[j])
        beta = jnp.broadcast_to(_sigmoid(b_col), g.shape)
        g = _chunk_cumsum(g, CHUNK)
        q3, k3, v3, g3, beta3 = c3(q), c3(k), c3(_lane_slab(v_all, j)), c3(g), c3(beta)
        g_last = g3[:, CHUNK - 1:CHUNK, :]

        p1 = g3.astype(BF16).astype(F32)
        p2 = (g3 - p1).astype(BF16).astype(F32)
        p3 = (g3 - p1 - p2).astype(BF16).astype(F32)
        pick = jnp.where((lane3 == 0) | (lane3 == 3), p1, jnp.where((lane3 == 1) | (lane3 == 4), p2, p3))
        col_op = jnp.where(lane3 < 3, pick, jnp.where(lane3 < 6, 1.0, 0.0))
        row_op = jnp.where(lane3 < 3, 1.0, jnp.where(lane3 < 6, -pick, 0.0))
        gdiff = _bmm_nt(col_op, row_op)
        decay = jnp.where(causal, jnp.exp(jnp.where(causal, gdiff, 0.0)), 0.0)

        kb3 = k3 * beta3
        kq_k = _bmm_nt(jnp.concatenate([kb3, q3], axis=1), k3)
        a_low = jnp.where(strict, kq_k[:, :CHUNK] * decay, 0.0)
        tinv = _neumann_inverse(-a_low, 5)
        eg = jnp.exp(g3)
        uw = _bmm(tinv, jnp.concatenate([v3 * beta3, kb3 * eg], axis=-1))
        streams.append(dict(
            u=uw[..., :GDN_HEAD_DIM], wq=jnp.concatenate([uw[..., GDN_HEAD_DIM:], q3 * eg], axis=1),
            attn=kq_k[:, CHUNK:] * decay, kd=k3 * jnp.exp(g_last - g3), gl=jnp.exp(g_last),
            s=s_ref[j], outs=[]))

    for c in range(nc):
        for st in streams:
            s = st['s']
            wq_s = _dot(st['wq'][c], s)
            v_new = st['u'][c] - wq_s[:CHUNK]
            st['outs'].append(wq_s[CHUNK:] + _dot(st['attn'][c], v_new))
            st['s'] = s * st['gl'][c] + _dot_tn(st['kd'][c], v_new)

    for j, st in enumerate(streams):
        s_ref[j] = st['s']
        o = jnp.concatenate(st['outs'], axis=0)
        o = o * lax.rsqrt(jnp.mean(o * o, axis=-1, keepdims=True) + 1e-6) * nw_ref[...]
        z = _lane_slab(z_ref[0], j)
        o_ref[0, :, j * LANES:(j + 1) * LANES] = (o * (z * _sigmoid(z))).astype(o_ref.dtype)


def _gdn(h3, conv_w, a_log, dt_bias, norm_w, d_delta, tb, group=STREAMS):
    b, t, _ = h3.shape
    nh = d_delta // GDN_HEAD_DIM
    tb = min(tb, t)
    group = min(group, nh)
    width = group * LANES
    hb = d_delta // width
    col = lambda off: pl.BlockSpec((1, tb, width), lambda bi, hi, ti, off=off: (bi, ti, off + hi))
    cw = lambda off: pl.BlockSpec((GDN_CONV, width), lambda bi, hi, ti, off=off: (0, off + hi))
    per_head = pl.BlockSpec((group, 1, LANES), lambda bi, hi, ti: (hi, 0, 0))
    alog_b = jnp.broadcast_to(a_log.astype(F32)[:, None, None], (nh, 1, LANES))
    dtb_b = jnp.broadcast_to(dt_bias.astype(F32)[:, None, None], (nh, 1, LANES))
    return pl.pallas_call(
        functools.partial(_gdn_kernel, tb=tb, n_heads=nh, group=group),
        grid=(b, nh // group, t // tb),
        in_specs=[col(0), col(hb), col(2 * hb), col(3 * hb),
                  pl.BlockSpec((1, tb, LANES), lambda bi, hi, ti: (bi, ti, 4 * d_delta // LANES)),
                  cw(0), cw(hb), cw(2 * hb), per_head, per_head,
                  pl.BlockSpec((1, LANES), lambda bi, hi, ti: (0, 0))],
        out_specs=pl.BlockSpec((1, tb, width), lambda bi, hi, ti: (bi, ti, hi)),
        out_shape=jax.ShapeDtypeStruct((b, t, d_delta), BF16),
        scratch_shapes=[pltpu.VMEM((3, HALO + tb, width), F32),
                        pltpu.VMEM((group, GDN_HEAD_DIM, GDN_HEAD_DIM), F32)],
        compiler_params=pltpu.CompilerParams(
            dimension_semantics=("parallel", "parallel", "arbitrary"), vmem_limit_bytes=VMEM_LIMIT),
        name="gdn_chunk",
    )(h3, h3, h3, h3, h3, conv_w, conv_w, conv_w, alog_b, dtb_b, norm_w.reshape(1, LANES))


def _rwkv_kernel(r_ref, k_ref, v_ref, xl_ref, mur_ref, muk_ref, muv_ref, mul_ref,
                 w0_ref, a0_ref, kk_ref, ka_ref, rk_ref, lg_ref, lb_ref,
                 w2_ref, a2_ref, g2_ref, o_ref, ext_ref, extl_ref, s_ref, *, tb, group):
    t_idx = pl.program_id(2)
    nc = tb // CHUNK
    c2 = 2 * CHUNK
    width = group * LANES

    @pl.when(t_idx == 0)
    def _():
        ext_ref[:, 0:HALO, :] = jnp.zeros((3, HALO, width), F32)
        extl_ref[0:HALO, :] = jnp.zeros((HALO, LORA_PAD), F32)
        s_ref[...] = jnp.zeros_like(s_ref)

    def shift(ext, x, mu):
        ext[HALO:HALO + tb, :] = x
        prev = ext[HALO - 1:HALO - 1 + tb, :]
        ext[0:HALO, :] = ext[tb:tb + HALO, :]
        return x + mu * (prev - x)

    r = shift(ext_ref.at[0], r_ref[0], mur_ref[...])
    k = shift(ext_ref.at[1], k_ref[0], muk_ref[...])
    v = shift(ext_ref.at[2], v_ref[0], muv_ref[...])
    xl = shift(extl_ref, xl_ref[0], mul_ref[...])

    w_log = -_softplus(-(w0_ref[...] + _dot(jnp.tanh(xl), w2_ref[0]))) - 0.5
    ld = -jnp.exp(w_log)
    a = _sigmoid(a0_ref[...] + _dot(xl, a2_ref[0]))
    gate = _dot(_sigmoid(xl), g2_ref[0])

    head0 = lax.broadcasted_iota(jnp.int32, (tb, LANES), 1) < RWKV_HEAD_DIM

    def head_sum(x):
        parts = []
        for j in range(group):
            xj = _lane_slab(x, j)
            s0 = jnp.sum(jnp.where(head0, xj, 0.0), axis=-1, keepdims=True)
            s1 = jnp.sum(jnp.where(head0, 0.0, xj), axis=-1, keepdims=True)
            parts.append(jnp.where(head0, s0, s1))
        return jnp.concatenate(parts, axis=-1) if group > 1 else parts[0]

    kk = k * kk_ref[...]
    kk = kk * lax.rsqrt(head_sum(kk * kk) + 1e-6)
    k2 = k * (1.0 + (a - 1.0) * ka_ref[...])
    bb = kk * a

    lc = _chunk_cumsum(ld, CHUNK)
    c3 = lambda t: t.reshape(nc, CHUNK, t.shape[-1])
    lc3 = c3(lc)
    l_last = lc3[:, CHUNK - 1:CHUNK, :]
    e_neg = jnp.exp(-lc)
    a_t = c3(-kk * jnp.exp(lc - ld))
    b_t = c3(bb * e_neg)
    k_t = c3(k2 * e_neg)
    r_t = c3(r * jnp.exp(lc))
    e_tail = jnp.exp(l_last - lc3)
    b_h = c3(bb) * e_tail
    k_h = c3(k2) * e_tail
    w_c = jnp.exp(l_last)
    v3 = c3(v)

    head0_3 = lax.broadcasted_iota(jnp.int32, (nc, CHUNK, LANES), 2) < RWKV_HEAD_DIM
    stack = lambda x: jnp.concatenate([jnp.where(head0_3, x, 0.0), jnp.where(head0_3, 0.0, x)], axis=1)
    dup = lambda x: jnp.concatenate([x, x], axis=1)
    ri = lax.broadcasted_iota(jnp.int32, (c2, c2), 0)
    ci = lax.broadcasted_iota(jnp.int32, (c2, c2), 1)
    same = (ri >= CHUNK) == (ci >= CHUNK)
    strict = (same & (ri > ci))[None]
    incl = (same & (ri >= ci))[None]
    h0c = lax.broadcasted_iota(jnp.int32, (CHUNK, LANES), 1) < RWKV_HEAD_DIM
    bd = ((lax.broadcasted_iota(jnp.int32, (LANES, LANES), 0) >= RWKV_HEAD_DIM)
          == (lax.broadcasted_iota(jnp.int32, (LANES, LANES), 1) >= RWKV_HEAD_DIM))

    streams = []
    for j in range(group):
        sl = lambda x: _lane_slab(x, j)
        a_s, r_s = stack(sl(a_t)), stack(sl(r_t))
        vv2 = dup(sl(v3))
        ar_bk = _bmm_nt(jnp.concatenate([a_s, r_s], axis=1),
                        jnp.concatenate([dup(sl(b_t)), dup(sl(k_t))], axis=1))
        m_ab = jnp.where(strict, ar_bk[:, :c2, :c2], 0.0)
        m_ak = jnp.where(strict, ar_bk[:, :c2, c2:], 0.0)
        m_rb = jnp.where(incl, ar_bk[:, c2:, :c2], 0.0)
        m_rk = jnp.where(incl, ar_bk[:, c2:, c2:], 0.0)
        tinv = _neumann_inverse(m_ab, 5)
        mv = _bmm(jnp.concatenate([m_ak, m_rk], axis=1), vv2)
        pt = _bmm(tinv, jnp.concatenate([mv[:, :c2], a_s], axis=-1))
        w_col = [jnp.broadcast_to(sl(w_c)[c], (LANES, LANES)).T for c in range(nc)]
        streams.append(dict(
            p=pt[..., :LANES], tr=jnp.concatenate([pt[..., LANES:], sl(r_t)], axis=1), yv=mv[:, c2:],
            m_rb=m_rb, v=sl(v3), bk=jnp.concatenate([sl(b_h), sl(k_h)], axis=1), w_col=w_col,
            s=s_ref[j], outs=[]))

    for c in range(nc):
        for st in streams:
            s = st['s']
            tr_s = _dot(st['tr'][c], s)
            u_s = st['p'][c] + tr_s[:c2]
            u = jnp.where(h0c, u_s[:CHUNK], u_s[CHUNK:])
            y_s = _dot(st['m_rb'][c], jnp.concatenate([u, u], axis=0)) + st['yv'][c]
            st['outs'].append(tr_s[c2:] + jnp.where(h0c, y_s[:CHUNK], y_s[CHUNK:]))
            upd = _dot_tn(st['bk'][c], jnp.concatenate([u, st['v'][c]], axis=0))
            st['s'] = s * st['w_col'][c] + jnp.where(bd, upd, 0.0)

    for j, st in enumerate(streams):
        s_ref[j] = st['s']
    y = jnp.concatenate([jnp.concatenate(st['outs'], axis=0) for st in streams], axis=-1)

    inv_n = 1.0 / RWKV_HEAD_DIM
    yc = y - head_sum(y) * inv_n
    y = yc * lax.rsqrt(head_sum(yc * yc) * inv_n + RWKV_LNX_EPS)
    y = y * lg_ref[...] + lb_ref[...]
    y = y + head_sum(r * k2 * rk_ref[...]) * v
    o_ref[0] = (y * gate).astype(o_ref.dtype)


def _rwkv(h3, col0, d_rwkv, mu, w0, w2p, a0, a2p, g2p, k_k, k_a, r_k, lnx_g, lnx_b, layer, tb, group=STREAMS):
    b, t, _ = h3.shape
    tb = min(tb, t)
    group = min(group, d_rwkv // LANES)
    width = group * LANES
    nblk = d_rwkv // width
    cb = col0 // width
    lora_blk = (col0 + 3 * d_rwkv) // LORA_PAD
    col = lambda off: pl.BlockSpec((1, tb, width), lambda bi, hi, ti, off=off: (bi, ti, cb + off + hi))
    vec = lambda off: pl.BlockSpec((1, width), lambda bi, hi, ti, off=off: (0, off + hi))
    lw = pl.BlockSpec((1, LORA_PAD, width), lambda bi, hi, ti: (layer, 0, hi))
    row = lambda x: x.astype(F32).reshape(1, -1)
    mu2 = row(mu)
    return pl.pallas_call(
        functools.partial(_rwkv_kernel, tb=tb, group=group),
        grid=(b, nblk, t // tb),
        in_specs=[col(0), col(nblk), col(2 * nblk),
                  pl.BlockSpec((1, tb, LORA_PAD), lambda bi, hi, ti: (bi, ti, lora_blk)),
                  vec(0), vec(nblk), vec(2 * nblk),
                  pl.BlockSpec((1, LORA_PAD), lambda bi, hi, ti: (0, 3 * d_rwkv // LORA_PAD)),
                  vec(0), vec(0), vec(0), vec(0), vec(0), vec(0), vec(0),
                  lw, lw, lw],
        out_specs=pl.BlockSpec((1, tb, width), lambda bi, hi, ti: (bi, ti, hi)),
        out_shape=jax.ShapeDtypeStruct((b, t, d_rwkv), BF16),
        scratch_shapes=[pltpu.VMEM((3, HALO + tb, width), F32),
                        pltpu.VMEM((HALO + tb, LORA_PAD), F32),
                        pltpu.VMEM((group, LANES, LANES), F32)],
        compiler_params=pltpu.CompilerParams(
            dimension_semantics=("parallel", "parallel", "arbitrary"), vmem_limit_bytes=VMEM_LIMIT),
        name="rwkv_chunk",
    )(h3, h3, h3, h3, mu2, mu2, mu2, mu2, row(w0), row(a0), row(k_k), row(k_a), row(r_k),
      row(lnx_g), row(lnx_b), w2p, a2p, g2p)


def _ln_router_kernel(mix_ref, x_ref, g_ref, b_ref, wh_ref, wl_ref, br_ref, x1_ref, x1b_ref, rt_ref):
    x1 = _layer_norm(DN_ALPHA * x_ref[...] + mix_ref[...], g_ref[...], b_ref[...])
    x1_ref[...] = x1
    x1b_ref[...] = x1.astype(BF16)
    hi, lo = _split_bf16(x1)
    d = lambda a, w: jnp.dot(a, w, preferred_element_type=F32)
    logits = d(hi, wh_ref[...]) + d(lo, wh_ref[...]) + d(hi, wl_ref[...]) + br_ref[...]

    lane_i = lax.broadcasted_iota(jnp.int32, logits.shape, 1)
    lane = lane_i.astype(F32)
    big = float(LANES)
    neg = -jnp.inf
    gmask = lane_i < N_GROUPS
    lg = jnp.where(gmask, logits, neg)
    m = jnp.max(lg, axis=-1, keepdims=True)
    p_grp = 1.0 / jnp.sum(jnp.where(gmask, jnp.exp(lg - m), 0.0), axis=-1, keepdims=True)
    g_sel = jnp.min(jnp.where(gmask & (lg == m), lane, big), axis=-1, keepdims=True)
    lane_grp = lax.shift_right_arithmetic(lane_i - N_GROUPS, EXPERTS_PER_GROUP.bit_length() - 1)
    emask = (lane_i >= N_GROUPS) & (lane_i < N_GROUPS + N_EXPERTS) & (lane_grp == g_sel.astype(jnp.int32))
    le = jnp.where(emask, logits, neg)
    v1 = jnp.max(le, axis=-1, keepdims=True)
    i1 = jnp.min(jnp.where(emask & (le == v1), lane, big), axis=-1, keepdims=True)
    emask2 = emask & (lane != i1)
    le2 = jnp.where(emask2, logits, neg)
    v2 = jnp.max(le2, axis=-1, keepdims=True)
    i2 = jnp.min(jnp.where(emask2 & (le2 == v2), lane, big), axis=-1, keepdims=True)
    t = jnp.exp(v2 - v1)
    den = 1.0 + t
    gate1 = p_grp / den
    gate2 = p_grp * t / den
    rt_ref[...] = jnp.where(lane_i == 0, i1 - N_GROUPS,
                            jnp.where(lane_i == 1, i2 - N_GROUPS,
                                      jnp.where(lane_i == 2, gate1, jnp.where(lane_i == 3, gate2, 0.0))))


def _ln_router(mix, x, g, b, wr_hi, wr_lo, br, tm=256):
    m, d = x.shape
    tm = min(tm, m)
    rows = lambda w: pl.BlockSpec((tm, w), lambda i: (i, 0))
    full = lambda s: pl.BlockSpec(s, lambda i: (0, 0))
    return pl.pallas_call(
        _ln_router_kernel,
        grid=(m // tm,),
        in_specs=[rows(d), rows(d), full((1, d)), full((1, d)),
                  full((d, LANES)), full((d, LANES)), full((1, LANES))],
        out_specs=[rows(d), rows(d), rows(LANES)],
        out_shape=[jax.ShapeDtypeStruct((m, d), F32), jax.ShapeDtypeStruct((m, d), BF16),
                   jax.ShapeDtypeStruct((m, LANES), F32)],
        compiler_params=pltpu.CompilerParams(
            dimension_semantics=("parallel",), vmem_limit_bytes=VMEM_LIMIT),
        name="ln_router",
    )(mix, x, g.reshape(1, d), b.reshape(1, d), wr_hi, wr_lo, br)


def _moe_kernel(te_ref, nt_ref, ts_ref, tl_ref, order_ref, x_hbm, wg_ref, wu_ref, wd_ref,
                y_hbm, xbuf, ybuf, sem_in, sem_out, *, tm):
    i = pl.program_id(0)
    n_tiles = nt_ref[0]
    slot = lax.rem(i, 2)
    n_tok = x_hbm.shape[0]
    n_pairs = y_hbm.shape[0] - tm

    def pairs_of(tile):
        start, length = ts_ref[tile], tl_ref[tile]
        for r in range(tm):
            valid = r < length
            yield r, valid, order_ref[jnp.where(valid, start + r, 0)]

    def gather(tile, sl):
        for r, valid, p in pairs_of(tile):
            tok = jnp.where(valid, jnp.where(p >= n_tok, p - n_tok, p), 0)
            pltpu.make_async_copy(x_hbm.at[pl.ds(tok, 1)], xbuf.at[sl, pl.ds(r, 1)], sem_in.at[sl]).start()

    def scatter_wait():
        pltpu.make_async_copy(ybuf, y_hbm.at[pl.ds(0, tm)], sem_out).wait()

    @pl.when(i == 0)
    def _():
        gather(0, 0)

    @pl.when(i + 1 < n_tiles)
    def _():
        gather(i + 1, 1 - slot)

    @pl.when(i < n_tiles)
    def _():
        pltpu.make_async_copy(x_hbm.at[pl.ds(0, tm)], xbuf.at[slot], sem_in.at[slot]).wait()
        xb = xbuf[slot].astype(BF16)
        hg = jnp.dot(xb, wg_ref[0, 0].astype(BF16), preferred_element_type=F32)
        hu = jnp.dot(xb, wu_ref[0, 0].astype(BF16), preferred_element_type=F32)
        hact = hg * _sigmoid(hg) * hu
        y = jnp.dot(hact.astype(BF16), wd_ref[0, 0].astype(BF16), preferred_element_type=F32)

        @pl.when(i > 0)
        def _():
            scatter_wait()

        ybuf[...] = y

        for r, valid, p in pairs_of(i):
            dst = jnp.where(valid, p, n_pairs + r)
            pltpu.make_async_copy(ybuf.at[pl.ds(r, 1)], y_hbm.at[pl.ds(dst, 1)], sem_out).start()

        @pl.when(i == n_tiles - 1)
        def _():
            scatter_wait()
            fill = pltpu.make_async_copy(ybuf, y_hbm.at[pl.ds(y_hbm.shape[0] - tm, tm)], sem_out)
            fill.start()
            fill.wait()


def _moe(x1, route, wg, wu, wd, layer, tm):
    m, d = x1.shape
    n_e, f = wg.shape[1], wg.shape[3]
    tm = min(tm, m)
    pairs = 2 * m
    nt_max = pairs // tm + n_e

    e_flat = jnp.concatenate([route[:, 0], route[:, 1]]).astype(jnp.int32)
    order = jnp.argsort(e_flat, stable=True).astype(jnp.int32)
    cnt = jnp.sum(e_flat[:, None] == jnp.arange(n_e, dtype=jnp.int32)[None, :], axis=0, dtype=jnp.int32)
    e_tiles = (cnt + tm - 1) // tm
    tile_end = jnp.cumsum(e_tiles)
    n_tiles = tile_end[-1:].astype(jnp.int32)
    tile_idx = jnp.arange(nt_max, dtype=jnp.int32)
    tile_e = jnp.minimum(jnp.searchsorted(tile_end, tile_idx, side='right'), n_e - 1).astype(jnp.int32)
    k_in_e = tile_idx - (tile_end - e_tiles)[tile_e]
    tile_start = ((jnp.cumsum(cnt) - cnt)[tile_e] + k_in_e * tm).astype(jnp.int32)
    tile_len = jnp.clip(cnt[tile_e] - k_in_e * tm, 0, tm).astype(jnp.int32)

    w_in_spec = pl.BlockSpec((1, 1, d, f), lambda i, te, nt, ts, tl, o: (layer, te[i], 0, 0))
    grid_spec = pltpu.PrefetchScalarGridSpec(
        num_scalar_prefetch=5,
        grid=(nt_max,),
        in_specs=[pl.BlockSpec(memory_space=pl.ANY), w_in_spec, w_in_spec,
                  pl.BlockSpec((1, 1, f, d), lambda i, te, nt, ts, tl, o: (layer, te[i], 0, 0))],
        out_specs=pl.BlockSpec(memory_space=pl.ANY),
        scratch_shapes=[pltpu.VMEM((2, tm, d), F32), pltpu.VMEM((tm, d), F32),
                        pltpu.SemaphoreType.DMA((2,)), pltpu.SemaphoreType.DMA],
    )
    return pl.pallas_call(
        functools.partial(_moe_kernel, tm=tm),
        grid_spec=grid_spec,
        out_shape=jax.ShapeDtypeStruct((pairs + tm, d), F32),
        compiler_params=pltpu.CompilerParams(
            dimension_semantics=("arbitrary",), vmem_limit_bytes=MOE_VMEM_LIMIT),
        name="moe_experts",
    )(tile_e, n_tiles, tile_start, tile_len, order, x1, wg, wu, wd)


def _ple_ln_kernel(x1_ref, x1b_ref, y0_ref, y1_ref, rt_ref, p_ref, wp_ref, gd_ref, gu_ref, g_ref, b_ref,
                   x2_ref, x2b_ref):
    d = lambda a, w: jnp.dot(a, w, preferred_element_type=F32)
    proj = d(p_ref[...].astype(BF16), wp_ref[0])
    gate = _sigmoid(d(d(x1b_ref[...], gd_ref[0]).astype(BF16), gu_ref[0]))
    rt = rt_ref[...]
    ffn = rt[:, 2:3] * y0_ref[...] + rt[:, 3:4] * y1_ref[...]
    y = DN_ALPHA * x1_ref[...] + ffn + proj * gate
    x2 = _layer_norm(y, g_ref[...], b_ref[...])
    x2_ref[...] = x2
    x2b_ref[...] = x2.astype(BF16)


def _ple_ln(x1, x1b, y, route, p, wp, gd, gu, g, b, layer, tm=128):
    m, d = x1.shape
    pd = p.shape[1]
    tm = min(tm, m)
    nb = m // tm
    rows = lambda w: pl.BlockSpec((tm, w), lambda i: (i, 0))
    full = lambda s: pl.BlockSpec(s, lambda i: (0, 0))
    stack = lambda r, c: pl.BlockSpec((1, r, c), lambda i: (layer, 0, 0))
    return pl.pallas_call(
        _ple_ln_kernel,
        grid=(nb,),
        in_specs=[rows(d), rows(d), rows(d), pl.BlockSpec((tm, d), lambda i: (i + nb, 0)), rows(LANES),
                  rows(pd), stack(pd, d), stack(d, pd), stack(pd, d), full((1, d)), full((1, d))],
        out_specs=[rows(d), rows(d)],
        out_shape=[jax.ShapeDtypeStruct((m, d), F32), jax.ShapeDtypeStruct((m, d), BF16)],
        compiler_params=pltpu.CompilerParams(
            dimension_semantics=("parallel",), vmem_limit_bytes=VMEM_LIMIT),
        name="ple_ln",
    )(x1, x1b, y, y, route, p, wp, gd, gu, g.reshape(1, d), b.reshape(1, d))


def _pad_rows(w, rows, offset):
    return jnp.pad(w, ((0, 0), (offset, rows - offset - w.shape[1]), (0, 0))).astype(BF16)


def _layer(layer, x, xb, p, w_p, conv_w, a_log, dt_bias, norm_w, mu, w0, w2p, a0, a2p, g2p, k_k, k_a, r_k,
           lnx_g, lnx_b, w_out, ln1_g, ln1_b, w_grp, b_grp, w_rt, b_rt, wg, wu, wd,
           wp, gd, gu, ln2_g, ln2_b, *, batch, tb_gdn=1024, tb_rwkv=512):
    m, d_model = x.shape
    t = m // batch
    d_mix = w_out.shape[1]
    d_delta = d_mix // 2
    d_rwkv = d_mix - d_delta
    col_rwkv = 4 * d_delta + AB_PAD

    h = _matmul(xb, w_p, layer)
    h3 = h.reshape(batch, t, h.shape[1])
    o_gdn = _gdn(h3, conv_w, a_log, dt_bias, norm_w, d_delta, tb_gdn)
    o_rwkv = _rwkv(h3, col_rwkv, d_rwkv, mu, w0, w2p, a0, a2p, g2p, k_k, k_a, r_k, lnx_g, lnx_b,
                   layer, tb_rwkv)
    mix_in = jnp.concatenate([o_gdn, o_rwkv], axis=-1).reshape(m, d_mix)
    mix = _matmul(mix_in, w_out, layer)

    w_router = jnp.concatenate(
        [w_grp, w_rt, jnp.zeros((d_model, LANES - N_GROUPS - N_EXPERTS), F32)], axis=1).astype(F32)
    wr_hi, wr_lo = _split_bf16(w_router)
    b_router = jnp.concatenate(
        [b_grp, b_rt, jnp.zeros((LANES - N_GROUPS - N_EXPERTS,), F32)]).astype(F32).reshape(1, LANES)
    x1, x1b, route = _ln_router(mix, x, ln1_g, ln1_b, wr_hi, wr_lo, b_router)

    y = _moe(x1, route, wg, wu, wd, layer, MOE_TILE)
    return _ple_ln(x1, x1b, y, route, p, wp, gd, gu, ln2_g, ln2_b, layer)


def kernel(x, p, w_in, gdn_conv_w, gdn_a_log, gdn_dt_bias, gdn_norm_w, rwkv_mu, rwkv_w0, rwkv_w2, rwkv_a0, rwkv_a2, rwkv_g2, rwkv_k_k, rwkv_k_a, rwkv_r_k, rwkv_lnx_g, rwkv_lnx_b, w_out, ln1_g, ln1_b, moe_w_grp, moe_b_grp, moe_w_rt, moe_b_rt, moe_w_gate, moe_w_up, moe_w_down, ple_w_proj, ple_w_gate_down, ple_w_gate_up, ln2_g, ln2_b):
    batch, t, d_model = x.shape
    m = batch * t
    xf = x.reshape(m, d_model).astype(F32)
    xb = xf.astype(BF16)

    n_layers = w_in.shape[0]
    w_p = _relayout_w_in(w_in, w_out.shape[1] // 2)
    w_out_b = w_out.astype(BF16)
    wp_b, gd_b, gu_b = (w.astype(BF16) for w in (ple_w_proj, ple_w_gate_down, ple_w_gate_up))
    lora_w, lora_a = rwkv_w2.shape[1], rwkv_a2.shape[1]
    w2p = _pad_rows(rwkv_w2, LORA_PAD, 0)
    a2p = _pad_rows(rwkv_a2, LORA_PAD, lora_w)
    g2p = _pad_rows(rwkv_g2, LORA_PAD, lora_w + lora_a)

    for i in range(n_layers):
        xf, xb = _layer(
            i, xf, xb, p[i].reshape(m, -1), w_p, gdn_conv_w[i], gdn_a_log[i], gdn_dt_bias[i],
            gdn_norm_w[i], rwkv_mu[i], rwkv_w0[i], w2p, rwkv_a0[i], a2p, g2p,
            rwkv_k_k[i], rwkv_k_a[i], rwkv_r_k[i], rwkv_lnx_g[i], rwkv_lnx_b[i], w_out_b,
            ln1_g[i], ln1_b[i], moe_w_grp[i], moe_b_grp[i], moe_w_rt[i], moe_b_rt[i],
            moe_w_gate, moe_w_up, moe_w_down, wp_b, gd_b, gu_b, ln2_g[i], ln2_b[i], batch=batch)
    return xf.reshape(batch, t, d_model)
```

```python
import functools

import jax
import jax.numpy as jnp
from jax import lax
from jax.experimental import pallas as pl
from jax.experimental.pallas import tpu as pltpu

F32 = jnp.float32
BF16 = jnp.bfloat16

DEPTH = 4
DN_ALPHA = (2 * DEPTH) ** 0.25
CHUNK = 64
LANES = 128
GDN_HEAD_DIM = 128
GDN_CONV = 4
RWKV_HEAD_DIM = 64
RWKV_LNX_EPS = 64e-5
N_GROUPS = 4
EXPERTS_PER_GROUP = 8
N_EXPERTS = N_GROUPS * EXPERTS_PER_GROUP
LORA_PAD = 256
AB_PAD = 256
HALO = 8
MOE_TILE = 256
PROJ_TILE = 512
ROW_ALIGN = 8
STREAMS = 2
VMEM_LIMIT = 48 * 1024 * 1024
MOE_VMEM_LIMIT = 58 * 1024 * 1024


def _sigmoid(x):
    return 1.0 / (1.0 + jnp.exp(-x))


def _softplus(x):
    return jnp.maximum(x, 0.0) + jnp.log(1.0 + jnp.exp(-jnp.abs(x)))


def _split_bf16(x):
    hi = x.astype(BF16)
    lo = (x - hi.astype(F32)).astype(BF16)
    return hi, lo


def _bmm(a, b):
    return jnp.einsum('nik,nkj->nij', a.astype(BF16), b.astype(BF16), preferred_element_type=F32)


def _bmm_nt(a, b):
    return jnp.einsum('nik,njk->nij', a.astype(BF16), b.astype(BF16), preferred_element_type=F32)


def _dot(a, b):
    return jnp.dot(a.astype(BF16), b.astype(BF16), preferred_element_type=F32)


def _dot_tn(a, b):
    return lax.dot_general(a.astype(BF16), b.astype(BF16), (((0,), (0,)), ((), ())),
                           preferred_element_type=F32)


def _neumann_inverse(m, steps):
    n = m.shape[-1]
    eye = (lax.broadcasted_iota(jnp.int32, (n, n), 0) == lax.broadcasted_iota(jnp.int32, (n, n), 1))
    p = m + eye.astype(F32)[None]
    mp = _bmm(m, m)
    for _ in range(steps - 1):
        both = _bmm(jnp.concatenate([p, mp], axis=1), mp)
        p = p + both[:, :n]
        mp = both[:, n:]
    return p + _bmm(p, mp)


def _chunk_cumsum(x, chunk):
    row = lax.broadcasted_iota(jnp.int32, x.shape, 0) & (chunk - 1)
    s = 1
    while s < chunk:
        x = x + jnp.where(row >= s, pltpu.roll(x, s, 0), 0.0)
        s *= 2
    return x


def _layer_norm(y, g, b):
    yc = y - jnp.mean(y, axis=-1, keepdims=True)
    var = jnp.mean(yc * yc, axis=-1, keepdims=True)
    return yc * lax.rsqrt(var + 1e-5) * g + b


def _lane_slab(x, j):
    return x[..., j * LANES:(j + 1) * LANES]


def _mm_kernel(x_ref, w_ref, o_ref):
    o_ref[...] = jnp.dot(x_ref[...], w_ref[0], preferred_element_type=F32)


def _matmul(x, w, layer, tm=1024, tn=PROJ_TILE):
    m, k = x.shape
    n = w.shape[2]
    tm, tn = min(tm, m), min(tn, n)
    return pl.pallas_call(
        _mm_kernel,
        grid=(m // tm, n // tn),
        in_specs=[pl.BlockSpec((tm, k), lambda i, j: (i, 0)),
                  pl.BlockSpec((1, k, tn), lambda i, j: (layer, 0, j))],
        out_specs=pl.BlockSpec((tm, tn), lambda i, j: (i, j)),
        out_shape=jax.ShapeDtypeStruct((m, n), F32),
        compiler_params=pltpu.CompilerParams(
            dimension_semantics=("parallel", "parallel"), vmem_limit_bytes=VMEM_LIMIT),
        name="proj_matmul",
    )(x, w)


def _mm_nt_kernel(x_ref, wt_ref, o_ref):
    o_ref[...] = lax.dot_general(x_ref[...], wt_ref[0], (((1,), (1,)), ((), ())),
                                 preferred_element_type=F32)


def _pick_tile(n, preferred):
    return next((t for t in preferred if n % t == 0), n)


def _matmul_nt(x, wt, layer, row0, n, tm, tn):
    m, k = x.shape
    tm = min(tm, m)
    return pl.pallas_call(
        _mm_nt_kernel,
        grid=(m // tm, n // tn),
        in_specs=[pl.BlockSpec((tm, k), lambda i, j: (i, 0)),
                  pl.BlockSpec((pl.Element(1), pl.Element(tn), pl.Element(k)),
                               lambda i, j: (layer, pl.multiple_of(row0 + j * tn, ROW_ALIGN), 0))],
        out_specs=pl.BlockSpec((tm, tn), lambda i, j: (i, j)),
        out_shape=jax.ShapeDtypeStruct((m, n), F32),
        compiler_params=pltpu.CompilerParams(
            dimension_semantics=("parallel", "parallel"), vmem_limit_bytes=VMEM_LIMIT),
        name="proj_matmul_nt",
    )(x, wt)


def _gdn_kernel(q_ref, k_ref, v_ref, z_ref, ab_ref, cwq_ref, cwk_ref, cwv_ref,
                alog_ref, dtb_ref, nw_ref, o_ref, ext_ref, s_ref, *, tb, n_heads, group):
    hg = pl.program_id(1)
    t_idx = pl.program_id(2)
    nc = tb // CHUNK
    width = group * LANES

    @pl.when(t_idx == 0)
    def _():
        ext_ref[:, 0:HALO, :] = jnp.zeros((3, HALO, width), F32)
        s_ref[...] = jnp.zeros_like(s_ref)

    def conv_silu(idx, x_ref, cw_ref):
        ext_ref[idx, HALO:HALO + tb, :] = x_ref[0]
        cw = cw_ref[...]
        acc = ext_ref[idx, HALO - 3:HALO - 3 + tb, :] * cw[0:1]
        for j in range(1, GDN_CONV):
            acc = acc + ext_ref[idx, HALO - 3 + j:HALO - 3 + j + tb, :] * cw[j:j + 1]
        ext_ref[idx, 0:HALO, :] = ext_ref[idx, tb:tb + HALO, :]
        return acc * _sigmoid(acc)

    q_all = conv_silu(0, q_ref, cwq_ref)
    k_all = conv_silu(1, k_ref, cwk_ref)
    v_all = conv_silu(2, v_ref, cwv_ref)
    ab = ab_ref[0]
    lane = lax.broadcasted_iota(jnp.int32, ab.shape, 1)
    ri = lax.broadcasted_iota(jnp.int32, (CHUNK, CHUNK), 0)
    ci = lax.broadcasted_iota(jnp.int32, (CHUNK, CHUNK), 1)
    causal = (ri >= ci)[None]
    strict = (ri > ci)[None]
    lane3 = lax.broadcasted_iota(jnp.int32, (nc, CHUNK, LANES), 2)
    c3 = lambda t: t.reshape(nc, CHUNK, t.shape[-1])

    streams = []
    for j in range(group):
        h = hg * group + j
        q = _lane_slab(q_all, j)
        k = _lane_slab(k_all, j)
        q = q * lax.rsqrt(jnp.sum(q * q, axis=-1, keepdims=True) + 1e-6) * (GDN_HEAD_DIM ** -0.5)
        k = k * lax.rsqrt(jnp.sum(k * k, axis=-1, keepdims=True) + 1e-6)
        a_col = jnp.sum(jnp.where(lane == h, ab, 0.0), axis=-1, keepdims=True)
        b_col = jnp.sum(jnp.where(lane == h + n_heads, ab, 0.0), axis=-1, keepdims=True)
        g = -jnp.exp(alog_ref[j]) * _softplus(a_col + dtb_ref[j])
        beta = jnp.broadcast_to(_sigmoid(b_col), g.shape)
        g = _chunk_cumsum(g, CHUNK)
        q3, k3, v3, g3, beta3 = c3(q), c3(k), c3(_lane_slab(v_all, j)), c3(g), c3(beta)
        g_last = g3[:, CHUNK - 1:CHUNK, :]

        p1 = g3.astype(BF16).astype(F32)
        p2 = (g3 - p1).astype(BF16).astype(F32)
        p3 = (g3 - p1 - p2).astype(BF16).astype(F32)
        pick = jnp.where((lane3 == 0) | (lane3 == 3), p1, jnp.where((lane3 == 1) | (lane3 == 4), p2, p3))
        col_op = jnp.where(lane3 < 3, pick, jnp.where(lane3 < 6, 1.0, 0.0))
        row_op = jnp.where(lane3 < 3, 1.0, jnp.where(lane3 < 6, -pick, 0.0))
        gdiff = _bmm_nt(col_op, row_op)
        decay = jnp.where(causal, jnp.exp(jnp.where(causal, gdiff, 0.0)), 0.0)

        kb3 = k3 * beta3
        kq_k = _bmm_nt(jnp.concatenate([kb3, q3], axis=1), k3)
        a_low = jnp.where(strict, kq_k[:, :CHUNK] * decay, 0.0)
        tinv = _neumann_inverse(-a_low, 5)
        eg = jnp.exp(g3)
        uw = _bmm(tinv, jnp.concatenate([v3 * beta3, kb3 * eg], axis=-1))
        streams.append(dict(
            u=uw[..., :GDN_HEAD_DIM], wq=jnp.concatenate([uw[..., GDN_HEAD_DIM:], q3 * eg], axis=1),
            attn=kq_k[:, CHUNK:] * decay, kd=k3 * jnp.exp(g_last - g3), gl=jnp.exp(g_last),
            s=s_ref[j], outs=[]))

    for c in range(nc):
        for st in streams:
            s = st['s']
            wq_s = _dot(st['wq'][c], s)
            v_new = st['u'][c] - wq_s[:CHUNK]
            st['outs'].append(wq_s[CHUNK:] + _dot(st['attn'][c], v_new))
            st['s'] = s * st['gl'][c] + _dot_tn(st['kd'][c], v_new)

    for j, st in enumerate(streams):
        s_ref[j] = st['s']
        o = jnp.concatenate(st['outs'], axis=0)
        o = o * lax.rsqrt(jnp.mean(o * o, axis=-1, keepdims=True) + 1e-6) * nw_ref[...]
        z = _lane_slab(z_ref[0], j)
        o_ref[0, :, j * LANES:(j + 1) * LANES] = (o * (z * _sigmoid(z))).astype(o_ref.dtype)


def _gdn(h3, conv_w, a_log, dt_bias, norm_w, d_delta, tb, group=STREAMS):
    b, t, _ = h3.shape
    nh = d_delta // GDN_HEAD_DIM
    tb = min(tb, t)
    group = min(group, nh)
    width = group * LANES
    hb = d_delta // width
    col = lambda off: pl.BlockSpec((1, tb, width), lambda bi, hi, ti, off=off: (bi, ti, off + hi))
    cw = lambda off: pl.BlockSpec((GDN_CONV, width), lambda bi, hi, ti, off=off: (0, off + hi))
    per_head = pl.BlockSpec((group, 1, LANES), lambda bi, hi, ti: (hi, 0, 0))
    alog_b = jnp.broadcast_to(a_log.astype(F32)[:, None, None], (nh, 1, LANES))
    dtb_b = jnp.broadcast_to(dt_bias.astype(F32)[:, None, None], (nh, 1, LANES))
    return pl.pallas_call(
        functools.partial(_gdn_kernel, tb=tb, n_heads=nh, group=group),
        grid=(b, nh // group, t // tb),
        in_specs=[col(0), col(hb), col(2 * hb), col(3 * hb),
                  pl.BlockSpec((1, tb, LANES), lambda bi, hi, ti: (bi, ti, 4 * d_delta // LANES)),
                  cw(0), cw(hb), cw(2 * hb), per_head, per_head,
                  pl.BlockSpec((1, LANES), lambda bi, hi, ti: (0, 0))],
        out_specs=pl.BlockSpec((1, tb, width), lambda bi, hi, ti: (bi, ti, hi)),
        out_shape=jax.ShapeDtypeStruct((b, t, d_delta), BF16),
        scratch_shapes=[pltpu.VMEM((3, HALO + tb, width), F32),
                        pltpu.VMEM((group, GDN_HEAD_DIM, GDN_HEAD_DIM), F32)],
        compiler_params=pltpu.CompilerParams(
            dimension_semantics=("parallel", "parallel", "arbitrary"), vmem_limit_bytes=VMEM_LIMIT),
        name="gdn_chunk",
    )(h3, h3, h3, h3, h3, conv_w, conv_w, conv_w, alog_b, dtb_b, norm_w.reshape(1, LANES))


def _rwkv_kernel(r_ref, k_ref, v_ref, xl_ref, mur_ref, muk_ref, muv_ref, mul_ref,
                 w0_ref, a0_ref, kk_ref, ka_ref, rk_ref, lg_ref, lb_ref,
                 w2_ref, a2_ref, g2_ref, o_ref, ext_ref, extl_ref, s_ref, *, tb, group):
    t_idx = pl.program_id(2)
    nc = tb // CHUNK
    c2 = 2 * CHUNK
    width = group * LANES

    @pl.when(t_idx == 0)
    def _():
        ext_ref[:, 0:HALO, :] = jnp.zeros((3, HALO, width), F32)
        extl_ref[0:HALO, :] = jnp.zeros((HALO, LORA_PAD), F32)
        s_ref[...] = jnp.zeros_like(s_ref)

    def shift(ext, x, mu):
        ext[HALO:HALO + tb, :] = x
        prev = ext[HALO - 1:HALO - 1 + tb, :]
        ext[0:HALO, :] = ext[tb:tb + HALO, :]
        return x + mu * (prev - x)

    r = shift(ext_ref.at[0], r_ref[0], mur_ref[...])
    k = shift(ext_ref.at[1], k_ref[0], muk_ref[...])
    v = shift(ext_ref.at[2], v_ref[0], muv_ref[...])
    xl = shift(extl_ref, xl_ref[0], mul_ref[...])

    w_log = -_softplus(-(w0_ref[...] + _dot(jnp.tanh(xl), w2_ref[0]))) - 0.5
    ld = -jnp.exp(w_log)
    a = _sigmoid(a0_ref[...] + _dot(xl, a2_ref[0]))
    gate = _dot(_sigmoid(xl), g2_ref[0])

    head0 = lax.broadcasted_iota(jnp.int32, (tb, LANES), 1) < RWKV_HEAD_DIM

    def head_sum(x):
        parts = []
        for j in range(group):
            xj = _lane_slab(x, j)
            s0 = jnp.sum(jnp.where(head0, xj, 0.0), axis=-1, keepdims=True)
            s1 = jnp.sum(jnp.where(head0, 0.0, xj), axis=-1, keepdims=True)
            parts.append(jnp.where(head0, s0, s1))
        return jnp.concatenate(parts, axis=-1) if group > 1 else parts[0]

    kk = k * kk_ref[...]
    kk = kk * lax.rsqrt(head_sum(kk * kk) + 1e-6)
    k2 = k * (1.0 + (a - 1.0) * ka_ref[...])
    bb = kk * a

    lc = _chunk_cumsum(ld, CHUNK)
    c3 = lambda t: t.reshape(nc, CHUNK, t.shape[-1])
    lc3 = c3(lc)
    l_last = lc3[:, CHUNK - 1:CHUNK, :]
    e_neg = jnp.exp(-lc)
    a_t = c3(-kk * jnp.exp(lc - ld))
    b_t = c3(bb * e_neg)
    k_t = c3(k2 * e_neg)
    r_t = c3(r * jnp.exp(lc))
    e_tail = jnp.exp(l_last - lc3)
    b_h = c3(bb) * e_tail
    k_h = c3(k2) * e_tail
    w_c = jnp.exp(l_last)
    v3 = c3(v)

    head0_3 = lax.broadcasted_iota(jnp.int32, (nc, CHUNK, LANES), 2) < RWKV_HEAD_DIM
    stack = lambda x: jnp.concatenate([jnp.where(head0_3, x, 0.0), jnp.where(head0_3, 0.0, x)], axis=1)
    dup = lambda x: jnp.concatenate([x, x], axis=1)
    ri = lax.broadcasted_iota(jnp.int32, (c2, c2), 0)
    ci = lax.broadcasted_iota(jnp.int32, (c2, c2), 1)
    same = (ri >= CHUNK) == (ci >= CHUNK)
    strict = (same & (ri > ci))[None]
    incl = (same & (ri >= ci))[None]
    h0c = lax.broadcasted_iota(jnp.int32, (CHUNK, LANES), 1) < RWKV_HEAD_DIM
    bd = ((lax.broadcasted_iota(jnp.int32, (LANES, LANES), 0) >= RWKV_HEAD_DIM)
          == (lax.broadcasted_iota(jnp.int32, (LANES, LANES), 1) >= RWKV_HEAD_DIM))

    streams = []
    for j in range(group):
        sl = lambda x: _lane_slab(x, j)
        a_s, r_s = stack(sl(a_t)), stack(sl(r_t))
        vv2 = dup(sl(v3))
        ar_bk = _bmm_nt(jnp.concatenate([a_s, r_s], axis=1),
                        jnp.concatenate([dup(sl(b_t)), dup(sl(k_t))], axis=1))
        m_ab = jnp.where(strict, ar_bk[:, :c2, :c2], 0.0)
        m_ak = jnp.where(strict, ar_bk[:, :c2, c2:], 0.0)
        m_rb = jnp.where(incl, ar_bk[:, c2:, :c2], 0.0)
        m_rk = jnp.where(incl, ar_bk[:, c2:, c2:], 0.0)
        tinv = _neumann_inverse(m_ab, 5)
        mv = _bmm(jnp.concatenate([m_ak, m_rk], axis=1), vv2)
        pt = _bmm(tinv, jnp.concatenate([mv[:, :c2], a_s], axis=-1))
        w_col = [jnp.broadcast_to(sl(w_c)[c], (LANES, LANES)).T for c in range(nc)]
        streams.append(dict(
            p=pt[..., :LANES], tr=jnp.concatenate([pt[..., LANES:], sl(r_t)], axis=1), yv=mv[:, c2:],
            m_rb=m_rb, v=sl(v3), bk=jnp.concatenate([sl(b_h), sl(k_h)], axis=1), w_col=w_col,
            s=s_ref[j], outs=[]))

    for c in range(nc):
        for st in streams:
            s = st['s']
            tr_s = _dot(st['tr'][c], s)
            u_s = st['p'][c] + tr_s[:c2]
            u = jnp.where(h0c, u_s[:CHUNK], u_s[CHUNK:])
            y_s = _dot(st['m_rb'][c], jnp.concatenate([u, u], axis=0)) + st['yv'][c]
            st['outs'].append(tr_s[c2:] + jnp.where(h0c, y_s[:CHUNK], y_s[CHUNK:]))
            upd = _dot_tn(st['bk'][c], jnp.concatenate([u, st['v'][c]], axis=0))
            st['s'] = s * st['w_col'][c] + jnp.where(bd, upd, 0.0)

    for j, st in enumerate(streams):
        s_ref[j] = st['s']
    y = jnp.concatenate([jnp.concatenate(st['outs'], axis=0) for st in streams], axis=-1)

    inv_n = 1.0 / RWKV_HEAD_DIM
    yc = y - head_sum(y) * inv_n
    y = yc * lax.rsqrt(head_sum(yc * yc) * inv_n + RWKV_LNX_EPS)
    y = y * lg_ref[...] + lb_ref[...]
    y = y + head_sum(r * k2 * rk_ref[...]) * v
    o_ref[0] = (y * gate).astype(o_ref.dtype)


def _rwkv(h3, col0, d_rwkv, mu, w0, w2p, a0, a2p, g2p, k_k, k_a, r_k, lnx_g, lnx_b, layer, tb, group=STREAMS):
    b, t, _ = h3.shape
    tb = min(tb, t)
    group = min(group, d_rwkv // LANES)
    width = group * LANES
    nblk = d_rwkv // width
    cb = col0 // width
    lora_blk = (col0 + 3 * d_rwkv) // LORA_PAD
    col = lambda off: pl.BlockSpec((1, tb, width), lambda bi, hi, ti, off=off: (bi, ti, cb + off + hi))
    vec = lambda off: pl.BlockSpec((1, width), lambda bi, hi, ti, off=off: (0, off + hi))
    lw = pl.BlockSpec((1, LORA_PAD, width), lambda bi, hi, ti: (layer, 0, hi))
    row = lambda x: x.astype(F32).reshape(1, -1)
    mu2 = row(mu)
    return pl.pallas_call(
        functools.partial(_rwkv_kernel, tb=tb, group=group),
        grid=(b, nblk, t // tb),
        in_specs=[col(0), col(nblk), col(2 * nblk),
                  pl.BlockSpec((1, tb, LORA_PAD), lambda bi, hi, ti: (bi, ti, lora_blk)),
                  vec(0), vec(nblk), vec(2 * nblk),
                  pl.BlockSpec((1, LORA_PAD), lambda bi, hi, ti: (0, 3 * d_rwkv // LORA_PAD)),
                  vec(0), vec(0), vec(0), vec(0), vec(0), vec(0), vec(0),
                  lw, lw, lw],
        out_specs=pl.BlockSpec((1, tb, width), lambda bi, hi, ti: (bi, ti, hi)),
        out_shape=jax.ShapeDtypeStruct((b, t, d_rwkv), BF16),
        scratch_shapes=[pltpu.VMEM((3, HALO + tb, width), F32),
                        pltpu.VMEM((HALO + tb, LORA_PAD), F32),
                        pltpu.VMEM((group, LANES, LANES), F32)],
        compiler_params=pltpu.CompilerParams(
            dimension_semantics=("parallel", "parallel", "arbitrary"), vmem_limit_bytes=VMEM_LIMIT),
        name="rwkv_chunk",
    )(h3, h3, h3, h3, mu2, mu2, mu2, mu2, row(w0), row(a0), row(k_k), row(k_a), row(r_k),
      row(lnx_g), row(lnx_b), w2p, a2p, g2p)


def _ln_router_kernel(mix_ref, x_ref, g_ref, b_ref, wh_ref, wl_ref, br_ref, x1_ref, x1b_ref, rt_ref):
    x1 = _layer_norm(DN_ALPHA * x_ref[...] + mix_ref[...], g_ref[...], b_ref[...])
    x1_ref[...] = x1
    x1b_ref[...] = x1.astype(BF16)
    hi, lo = _split_bf16(x1)
    d = lambda a, w: jnp.dot(a, w, preferred_element_type=F32)
    logits = d(hi, wh_ref[...]) + d(lo, wh_ref[...]) + d(hi, wl_ref[...]) + br_ref[...]

    lane_i = lax.broadcasted_iota(jnp.int32, logits.shape, 1)
    lane = lane_i.astype(F32)
    big = float(LANES)
    neg = -jnp.inf
    gmask = lane_i < N_GROUPS
    lg = jnp.where(gmask, logits, neg)
    m = jnp.max(lg, axis=-1, keepdims=True)
    p_grp = 1.0 / jnp.sum(jnp.where(gmask, jnp.exp(lg - m), 0.0), axis=-1, keepdims=True)
    g_sel = jnp.min(jnp.where(gmask & (lg == m), lane, big), axis=-1, keepdims=True)
    lane_grp = lax.shift_right_arithmetic(lane_i - N_GROUPS, EXPERTS_PER_GROUP.bit_length() - 1)
    emask = (lane_i >= N_GROUPS) & (lane_i < N_GROUPS + N_EXPERTS) & (lane_grp == g_sel.astype(jnp.int32))
    le = jnp.where(emask, logits, neg)
    v1 = jnp.max(le, axis=-1, keepdims=True)
    i1 = jnp.min(jnp.where(emask & (le == v1), lane, big), axis=-1, keepdims=True)
    emask2 = emask & (lane != i1)
    le2 = jnp.where(emask2, logits, neg)
    v2 = jnp.max(le2, axis=-1, keepdims=True)
    i2 = jnp.min(jnp.where(emask2 & (le2 == v2), lane, big), axis=-1, keepdims=True)
    t = jnp.exp(v2 - v1)
    den = 1.0 + t
    gate1 = p_grp / den
    gate2 = p_grp * t / den
    rt_ref[...] = jnp.where(lane_i == 0, i1 - N_GROUPS,
                            jnp.where(lane_i == 1, i2 - N_GROUPS,
                                      jnp.where(lane_i == 2, gate1, jnp.where(lane_i == 3, gate2, 0.0))))


def _ln_router(mix, x, g, b, wr_hi, wr_lo, br, tm=256):
    m, d = x.shape
    tm = min(tm, m)
    rows = lambda w: pl.BlockSpec((tm, w), lambda i: (i, 0))
    full = lambda s: pl.BlockSpec(s, lambda i: (0, 0))
    return pl.pallas_call(
        _ln_router_kernel,
        grid=(m // tm,),
        in_specs=[rows(d), rows(d), full((1, d)), full((1, d)),
                  full((d, LANES)), full((d, LANES)), full((1, LANES))],
        out_specs=[rows(d), rows(d), rows(LANES)],
        out_shape=[jax.ShapeDtypeStruct((m, d), F32), jax.ShapeDtypeStruct((m, d), BF16),
                   jax.ShapeDtypeStruct((m, LANES), F32)],
        compiler_params=pltpu.CompilerParams(
            dimension_semantics=("parallel",), vmem_limit_bytes=VMEM_LIMIT),
        name="ln_router",
    )(mix, x, g.reshape(1, d), b.reshape(1, d), wr_hi, wr_lo, br)


def _moe_kernel(te_ref, nt_ref, ts_ref, tl_ref, order_ref, x_hbm, wg_ref, wu_ref, wd_ref,
                y_hbm, xbuf, ybuf, sem_in, sem_out, *, tm):
    i = pl.program_id(0)
    n_tiles = nt_ref[0]
    slot = lax.rem(i, 2)
    n_tok = x_hbm.shape[0]
    n_pairs = y_hbm.shape[0] - tm

    def pairs_of(tile):
        start, length = ts_ref[tile], tl_ref[tile]
        for r in range(tm):
            valid = r < length
            yield r, valid, order_ref[jnp.where(valid, start + r, 0)]

    def gather(tile, sl):
        for r, valid, p in pairs_of(tile):
            tok = jnp.where(valid, jnp.where(p >= n_tok, p - n_tok, p), 0)
            pltpu.make_async_copy(x_hbm.at[pl.ds(tok, 1)], xbuf.at[sl, pl.ds(r, 1)], sem_in.at[sl]).start()

    def scatter_wait():
        pltpu.make_async_copy(ybuf, y_hbm.at[pl.ds(0, tm)], sem_out).wait()

    @pl.when(i == 0)
    def _():
        gather(0, 0)

    @pl.when(i + 1 < n_tiles)
    def _():
        gather(i + 1, 1 - slot)

    @pl.when(i < n_tiles)
    def _():
        pltpu.make_async_copy(x_hbm.at[pl.ds(0, tm)], xbuf.at[slot], sem_in.at[slot]).wait()
        xb = xbuf[slot].astype(BF16)
        hg = jnp.dot(xb, wg_ref[0, 0].astype(BF16), preferred_element_type=F32)
        hu = jnp.dot(xb, wu_ref[0, 0].astype(BF16), preferred_element_type=F32)
        hact = hg * _sigmoid(hg) * hu
        y = jnp.dot(hact.astype(BF16), wd_ref[0, 0].astype(BF16), preferred_element_type=F32)

        @pl.when(i > 0)
        def _():
            scatter_wait()

        ybuf[...] = y

        for r, valid, p in pairs_of(i):
            dst = jnp.where(valid, p, n_pairs + r)
            pltpu.make_async_copy(ybuf.at[pl.ds(r, 1)], y_hbm.at[pl.ds(dst, 1)], sem_out).start()

        @pl.when(i == n_tiles - 1)
        def _():
            scatter_wait()
            fill = pltpu.make_async_copy(ybuf, y_hbm.at[pl.ds(y_hbm.shape[0] - tm, tm)], sem_out)
            fill.start()
            fill.wait()


def _moe(x1, route, wg, wu, wd, layer, tm):
    m, d = x1.shape
    n_e, f = wg.shape[1], wg.shape[3]
    tm = min(tm, m)
    pairs = 2 * m
    nt_max = pairs // tm + n_e

    e_flat = jnp.concatenate([route[:, 0], route[:, 1]]).astype(jnp.int32)
    order = jnp.argsort(e_flat, stable=True).astype(jnp.int32)
    cnt = jnp.sum(e_flat[:, None] == jnp.arange(n_e, dtype=jnp.int32)[None, :], axis=0, dtype=jnp.int32)
    e_tiles = (cnt + tm - 1) // tm
    tile_end = jnp.cumsum(e_tiles)
    n_tiles = tile_end[-1:].astype(jnp.int32)
    tile_idx = jnp.arange(nt_max, dtype=jnp.int32)
    tile_e = jnp.minimum(jnp.searchsorted(tile_end, tile_idx, side='right'), n_e - 1).astype(jnp.int32)
    k_in_e = tile_idx - (tile_end - e_tiles)[tile_e]
    tile_start = ((jnp.cumsum(cnt) - cnt)[tile_e] + k_in_e * tm).astype(jnp.int32)
    tile_len = jnp.clip(cnt[tile_e] - k_in_e * tm, 0, tm).astype(jnp.int32)

    w_in_spec = pl.BlockSpec((1, 1, d, f), lambda i, te, nt, ts, tl, o: (layer, te[i], 0, 0))
    grid_spec = pltpu.PrefetchScalarGridSpec(
        num_scalar_prefetch=5,
        grid=(nt_max,),
        in_specs=[pl.BlockSpec(memory_space=pl.ANY), w_in_spec, w_in_spec,
                  pl.BlockSpec((1, 1, f, d), lambda i, te, nt, ts, tl, o: (layer, te[i], 0, 0))],
        out_specs=pl.BlockSpec(memory_space=pl.ANY),
        scratch_shapes=[pltpu.VMEM((2, tm, d), F32), pltpu.VMEM((tm, d), F32),
                        pltpu.SemaphoreType.DMA((2,)), pltpu.SemaphoreType.DMA],
    )
    return pl.pallas_call(
        functools.partial(_moe_kernel, tm=tm),
        grid_spec=grid_spec,
        out_shape=jax.ShapeDtypeStruct((pairs + tm, d), F32),
        compiler_params=pltpu.CompilerParams(
            dimension_semantics=("arbitrary",), vmem_limit_bytes=MOE_VMEM_LIMIT),
        name="moe_experts",
    )(tile_e, n_tiles, tile_start, tile_len, order, x1, wg, wu, wd)


def _ple_ln_kernel(x1_ref, x1b_ref, y0_ref, y1_ref, rt_ref, p_ref, wp_ref, gd_ref, gu_ref, g_ref, b_ref,
                   x2_ref, x2b_ref):
    d = lambda a, w: jnp.dot(a, w, preferred_element_type=F32)
    proj = d(p_ref[...].astype(BF16), wp_ref[0])
    gate = _sigmoid(d(d(x1b_ref[...], gd_ref[0]).astype(BF16), gu_ref[0]))
    rt = rt_ref[...]
    ffn = rt[:, 2:3] * y0_ref[...] + rt[:, 3:4] * y1_ref[...]
    y = DN_ALPHA * x1_ref[...] + ffn + proj * gate
    x2 = _layer_norm(y, g_ref[...], b_ref[...])
    x2_ref[...] = x2
    x2b_ref[...] = x2.astype(BF16)


def _ple_ln(x1, x1b, y, route, p, wp, gd, gu, g, b, layer, tm=128):
    m, d = x1.shape
    pd = p.shape[1]
    tm = min(tm, m)
    nb = m // tm
    rows = lambda w: pl.BlockSpec((tm, w), lambda i: (i, 0))
    full = lambda s: pl.BlockSpec(s, lambda i: (0, 0))
    stack = lambda r, c: pl.BlockSpec((1, r, c), lambda i: (layer, 0, 0))
    return pl.pallas_call(
        _ple_ln_kernel,
        grid=(nb,),
        in_specs=[rows(d), rows(d), rows(d), pl.BlockSpec((tm, d), lambda i: (i + nb, 0)), rows(LANES),
                  rows(pd), stack(pd, d), stack(d, pd), stack(pd, d), full((1, d)), full((1, d))],
        out_specs=[rows(d), rows(d)],
        out_shape=[jax.ShapeDtypeStruct((m, d), F32), jax.ShapeDtypeStruct((m, d), BF16)],
        compiler_params=pltpu.CompilerParams(
            dimension_semantics=("parallel",), vmem_limit_bytes=VMEM_LIMIT),
        name="ple_ln",
    )(x1, x1b, y, y, route, p, wp, gd, gu, g.reshape(1, d), b.reshape(1, d))


def _pad_rows(w, rows, offset):
    return jnp.pad(w, ((0, 0), (offset, rows - offset - w.shape[1]), (0, 0))).astype(BF16)


def _layer(layer, x, xb, p, wt, conv_w, a_log, dt_bias, norm_w, mu, w0, w2p, a0, a2p, g2p,
           k_k, k_a, r_k, lnx_g, lnx_b, w_out, ln1_g, ln1_b, w_grp, b_grp, w_rt, b_rt, wg, wu, wd,
           wp, gd, gu, ln2_g, ln2_b, *, batch, tb_gdn=1024, tb_rwkv=512):
    m, d_model = x.shape
    t = m // batch
    d_mix = w_out.shape[1]
    d_delta = d_mix // 2
    d_rwkv = d_mix - d_delta

    n_ab = 2 * (d_delta // GDN_HEAD_DIM)
    n_gdn, n_rwkv = 4 * d_delta + AB_PAD, wt.shape[1] - 4 * d_delta - n_ab
    h_gdn = _matmul_nt(xb, wt, layer, 0, n_gdn, 1024, _pick_tile(n_gdn, (768, 512, 256)))
    h_rwkv = _matmul_nt(xb, wt, layer, 4 * d_delta + n_ab, n_rwkv, 512,
                        _pick_tile(n_rwkv, (1280, 896, 512, 256)))
    o_gdn = _gdn(h_gdn.reshape(batch, t, -1), conv_w, a_log, dt_bias, norm_w, d_delta, tb_gdn)
    o_rwkv = _rwkv(h_rwkv.reshape(batch, t, -1), 0, d_rwkv, mu, w0, w2p, a0, a2p, g2p, k_k, k_a, r_k,
                   lnx_g, lnx_b, layer, tb_rwkv)
    mix_in = jnp.concatenate([o_gdn, o_rwkv], axis=-1).reshape(m, d_mix)
    mix = _matmul(mix_in, w_out, layer)

    w_router = jnp.concatenate(
        [w_grp, w_rt, jnp.zeros((d_model, LANES - N_GROUPS - N_EXPERTS), F32)], axis=1).astype(F32)
    wr_hi, wr_lo = _split_bf16(w_router)
    b_router = jnp.concatenate(
        [b_grp, b_rt, jnp.zeros((LANES - N_GROUPS - N_EXPERTS,), F32)]).astype(F32).reshape(1, LANES)
    x1, x1b, route = _ln_router(mix, x, ln1_g, ln1_b, wr_hi, wr_lo, b_router)

    y = _moe(x1, route, wg, wu, wd, layer, MOE_TILE)
    return _ple_ln(x1, x1b, y, route, p, wp, gd, gu, ln2_g, ln2_b, layer)


def kernel(x, p, w_in, gdn_conv_w, gdn_a_log, gdn_dt_bias, gdn_norm_w, rwkv_mu, rwkv_w0, rwkv_w2, rwkv_a0, rwkv_a2, rwkv_g2, rwkv_k_k, rwkv_k_a, rwkv_r_k, rwkv_lnx_g, rwkv_lnx_b, w_out, ln1_g, ln1_b, moe_w_grp, moe_b_grp, moe_w_rt, moe_b_rt, moe_w_gate, moe_w_up, moe_w_down, ple_w_proj, ple_w_gate_down, ple_w_gate_up, ln2_g, ln2_b):
    batch, t, d_model = x.shape
    m = batch * t
    xf = x.reshape(m, d_model).astype(F32)
    xb = xf.astype(BF16)

    n_layers = w_in.shape[0]
    wt = jnp.swapaxes(w_in, 1, 2).astype(BF16)
    w_out_b = w_out.astype(BF16)
    wp_b, gd_b, gu_b = (w.astype(BF16) for w in (ple_w_proj, ple_w_gate_down, ple_w_gate_up))
    lora_w, lora_a = rwkv_w2.shape[1], rwkv_a2.shape[1]
    w2p = _pad_rows(rwkv_w2, LORA_PAD, 0)
    a2p = _pad_rows(rwkv_a2, LORA_PAD, lora_w)
    g2p = _pad_rows(rwkv_g2, LORA_PAD, lora_w + lora_a)

    for i in range(n_layers):
        xf, xb = _layer(
            i, xf, xb, p[i].reshape(m, -1), wt, gdn_conv_w[i], gdn_a_log[i], gdn_dt_bias[i],
            gdn_norm_w[i], rwkv_mu[i], rwkv_w0[i], w2p, rwkv_a0[i], a2p, g2p,
            rwkv_k_k[i], rwkv_k_a[i], rwkv_r_k[i], rwkv_lnx_g[i], rwkv_lnx_b[i], w_out_b,
            ln1_g[i], ln1_b[i], moe_w_grp[i], moe_b_grp[i], moe_w_rt[i], moe_b_rt[i],
            moe_w_gate, moe_w_up, moe_w_down, wp_b, gd_b, gu_b, ln2_g[i], ln2_b[i], batch=batch)
    return xf.reshape(batch, t, d_model)
```

```python
import functools

import jax
import jax.numpy as jnp
from jax import lax
from jax.experimental import pallas as pl
from jax.experimental.pallas import tpu as pltpu

F32 = jnp.float32
BF16 = jnp.bfloat16

DEPTH = 4
DN_ALPHA = (2 * DEPTH) ** 0.25
CHUNK = 64
LANES = 128
GDN_HEAD_DIM = 128
GDN_CONV = 4
RWKV_HEAD_DIM = 64
RWKV_LNX_EPS = 64e-5
N_GROUPS = 4
EXPERTS_PER_GROUP = 8
N_EXPERTS = N_GROUPS * EXPERTS_PER_GROUP
LORA_PAD = 256
AB_PAD = 256
HALO = 8
MOE_TILE = 256
PROJ_TILE = 512
ROW_ALIGN = 8
STREAMS = 2
VMEM_LIMIT = 48 * 1024 * 1024
MOE_VMEM_LIMIT = 58 * 1024 * 1024


def _sigmoid(x):
    return 1.0 / (1.0 + jnp.exp(-x))


def _softplus(x):
    return jnp.maximum(x, 0.0) + jnp.log(1.0 + jnp.exp(-jnp.abs(x)))


def _split_bf16(x):
    hi = x.astype(BF16)
    lo = (x - hi.astype(F32)).astype(BF16)
    return hi, lo


def _bmm(a, b):
    return jnp.einsum('nik,nkj->nij', a.astype(BF16), b.astype(BF16), preferred_element_type=F32)


def _bmm_nt(a, b):
    return jnp.einsum('nik,njk->nij', a.astype(BF16), b.astype(BF16), preferred_element_type=F32)


def _dot(a, b):
    return jnp.dot(a.astype(BF16), b.astype(BF16), preferred_element_type=F32)


def _dot_tn(a, b):
    return lax.dot_general(a.astype(BF16), b.astype(BF16), (((0,), (0,)), ((), ())),
                           preferred_element_type=F32)


def _neumann_inverse(m, steps):
    n = m.shape[-1]
    eye = (lax.broadcasted_iota(jnp.int32, (n, n), 0) == lax.broadcasted_iota(jnp.int32, (n, n), 1))
    p = m + eye.astype(F32)[None]
    mp = _bmm(m, m)
    for _ in range(steps - 1):
        both = _bmm(jnp.concatenate([p, mp], axis=1), mp)
        p = p + both[:, :n]
        mp = both[:, n:]
    return p + _bmm(p, mp)


def _chunk_cumsum(x, chunk):
    row = lax.broadcasted_iota(jnp.int32, x.shape, 0) & (chunk - 1)
    s = 1
    while s < chunk:
        x = x + jnp.where(row >= s, pltpu.roll(x, s, 0), 0.0)
        s *= 2
    return x


def _layer_norm(y, g, b):
    yc = y - jnp.mean(y, axis=-1, keepdims=True)
    var = jnp.mean(yc * yc, axis=-1, keepdims=True)
    return yc * lax.rsqrt(var + 1e-5) * g + b


def _lane_slab(x, j):
    return x[..., j * LANES:(j + 1) * LANES]


def _pack_halves(x):
    w = x.shape[-1] // 2
    bits = lambda t: lax.bitcast_convert_type(t.astype(BF16).astype(F32), jnp.uint32)
    return bits(x[:, :w]) | (bits(x[:, w:]) >> 16)


def _unpack_halves(p):
    hi = lax.bitcast_convert_type(p & jnp.uint32(0xFFFF0000), F32)
    lo = lax.bitcast_convert_type(p << 16, F32)
    return jnp.concatenate([hi, lo], axis=-1)


def _mm_kernel(x_ref, w_ref, o_ref):
    o_ref[...] = jnp.dot(x_ref[...], w_ref[0], preferred_element_type=F32)


def _matmul(x, w, layer, tm=1024, tn=PROJ_TILE):
    m, k = x.shape
    n = w.shape[2]
    tm, tn = min(tm, m), min(tn, n)
    return pl.pallas_call(
        _mm_kernel,
        grid=(m // tm, n // tn),
        in_specs=[pl.BlockSpec((tm, k), lambda i, j: (i, 0)),
                  pl.BlockSpec((1, k, tn), lambda i, j: (layer, 0, j))],
        out_specs=pl.BlockSpec((tm, tn), lambda i, j: (i, j)),
        out_shape=jax.ShapeDtypeStruct((m, n), F32),
        compiler_params=pltpu.CompilerParams(
            dimension_semantics=("parallel", "parallel"), vmem_limit_bytes=VMEM_LIMIT),
        name="proj_matmul",
    )(x, w)


def _mm_nt_kernel(x_ref, wt_ref, o_ref):
    o_ref[...] = lax.dot_general(x_ref[...], wt_ref[0], (((1,), (1,)), ((), ())),
                                 preferred_element_type=F32)


def _pick_tile(n, preferred):
    return next((t for t in preferred if n % t == 0), n)


def _matmul_nt(x, wt, layer, row0, n, tm, tn):
    m, k = x.shape
    tm = min(tm, m)
    return pl.pallas_call(
        _mm_nt_kernel,
        grid=(m // tm, n // tn),
        in_specs=[pl.BlockSpec((tm, k), lambda i, j: (i, 0)),
                  pl.BlockSpec((pl.Element(1), pl.Element(tn), pl.Element(k)),
                               lambda i, j: (layer, pl.multiple_of(row0 + j * tn, ROW_ALIGN), 0))],
        out_specs=pl.BlockSpec((tm, tn), lambda i, j: (i, j)),
        out_shape=jax.ShapeDtypeStruct((m, n), F32),
        compiler_params=pltpu.CompilerParams(
            dimension_semantics=("parallel", "parallel"), vmem_limit_bytes=VMEM_LIMIT),
        name="proj_matmul_nt",
    )(x, wt)


def _gdn_kernel(q_ref, k_ref, v_ref, z_ref, ab_ref, cwq_ref, cwk_ref, cwv_ref,
                alog_ref, dtb_ref, nw_ref, o_ref, ext_ref, s_ref, *, tb, n_heads, group):
    hg = pl.program_id(1)
    t_idx = pl.program_id(2)
    nc = tb // CHUNK
    width = group * LANES

    @pl.when(t_idx == 0)
    def _():
        ext_ref[:, 0:HALO, :] = jnp.zeros((3, HALO, width), F32)
        s_ref[...] = jnp.zeros_like(s_ref)

    def conv_silu(idx, x_ref, cw_ref):
        ext_ref[idx, HALO:HALO + tb, :] = x_ref[0]
        cw = cw_ref[...]
        acc = ext_ref[idx, HALO - 3:HALO - 3 + tb, :] * cw[0:1]
        for j in range(1, GDN_CONV):
            acc = acc + ext_ref[idx, HALO - 3 + j:HALO - 3 + j + tb, :] * cw[j:j + 1]
        ext_ref[idx, 0:HALO, :] = ext_ref[idx, tb:tb + HALO, :]
        return acc * _sigmoid(acc)

    q_all = conv_silu(0, q_ref, cwq_ref)
    k_all = conv_silu(1, k_ref, cwk_ref)
    v_all = conv_silu(2, v_ref, cwv_ref)
    ab = ab_ref[0]
    lane = lax.broadcasted_iota(jnp.int32, ab.shape, 1)
    ri = lax.broadcasted_iota(jnp.int32, (CHUNK, CHUNK), 0)
    ci = lax.broadcasted_iota(jnp.int32, (CHUNK, CHUNK), 1)
    causal = (ri >= ci)[None]
    strict = (ri > ci)[None]
    lane3 = lax.broadcasted_iota(jnp.int32, (nc, CHUNK, LANES), 2)
    c3 = lambda t: t.reshape(nc, CHUNK, t.shape[-1])

    streams = []
    for j in range(group):
        h = hg * group + j
        q = _lane_slab(q_all, j)
        k = _lane_slab(k_all, j)
        q = q * lax.rsqrt(jnp.sum(q * q, axis=-1, keepdims=True) + 1e-6) * (GDN_HEAD_DIM ** -0.5)
        k = k * lax.rsqrt(jnp.sum(k * k, axis=-1, keepdims=True) + 1e-6)
        a_col = jnp.sum(jnp.where(lane == h, ab, 0.0), axis=-1, keepdims=True)
        b_col = jnp.sum(jnp.where(lane == h + n_heads, ab, 0.0), axis=-1, keepdims=True)
        g = -jnp.exp(alog_ref[j]) * _softplus(a_col + dtb_ref[j])
        beta = jnp.broadcast_to(_sigmoid(b_col), g.shape)
        g = _chunk_cumsum(g, CHUNK)
        q3, k3, v3, g3, beta3 = c3(q), c3(k), c3(_lane_slab(v_all, j)), c3(g), c3(beta)
        g_last = g3[:, CHUNK - 1:CHUNK, :]

        p1 = g3.astype(BF16).astype(F32)
        p2 = (g3 - p1).astype(BF16).astype(F32)
        p3 = (g3 - p1 - p2).astype(BF16).astype(F32)
        pick = jnp.where((lane3 == 0) | (lane3 == 3), p1, jnp.where((lane3 == 1) | (lane3 == 4), p2, p3))
        col_op = jnp.where(lane3 < 3, pick, jnp.where(lane3 < 6, 1.0, 0.0))
        row_op = jnp.where(lane3 < 3, 1.0, jnp.where(lane3 < 6, -pick, 0.0))
        gdiff = _bmm_nt(col_op, row_op)
        decay = jnp.where(causal, jnp.exp(jnp.where(causal, gdiff, 0.0)), 0.0)

        kb3 = k3 * beta3
        kq_k = _bmm_nt(jnp.concatenate([kb3, q3], axis=1), k3)
        a_low = jnp.where(strict, kq_k[:, :CHUNK] * decay, 0.0)
        tinv = _neumann_inverse(-a_low, 5)
        eg = jnp.exp(g3)
        uw = _bmm(tinv, jnp.concatenate([v3 * beta3, kb3 * eg], axis=-1))
        streams.append(dict(
            u=uw[..., :GDN_HEAD_DIM], wq=jnp.concatenate([uw[..., GDN_HEAD_DIM:], q3 * eg], axis=1),
            attn=kq_k[:, CHUNK:] * decay, kd=k3 * jnp.exp(g_last - g3), gl=jnp.exp(g_last),
            s=s_ref[j], outs=[]))

    for c in range(nc):
        for st in streams:
            s = st['s']
            wq_s = _dot(st['wq'][c], s)
            v_new = st['u'][c] - wq_s[:CHUNK]
            st['outs'].append(wq_s[CHUNK:] + _dot(st['attn'][c], v_new))
            st['s'] = s * st['gl'][c] + _dot_tn(st['kd'][c], v_new)

    for j, st in enumerate(streams):
        s_ref[j] = st['s']
        o = jnp.concatenate(st['outs'], axis=0)
        o = o * lax.rsqrt(jnp.mean(o * o, axis=-1, keepdims=True) + 1e-6) * nw_ref[...]
        z = _lane_slab(z_ref[0], j)
        o_ref[0, :, j * LANES:(j + 1) * LANES] = (o * (z * _sigmoid(z))).astype(o_ref.dtype)


def _gdn(h3, conv_w, a_log, dt_bias, norm_w, d_delta, tb, group=STREAMS):
    b, t, _ = h3.shape
    nh = d_delta // GDN_HEAD_DIM
    tb = min(tb, t)
    group = min(group, nh)
    width = group * LANES
    hb = d_delta // width
    col = lambda off: pl.BlockSpec((1, tb, width), lambda bi, hi, ti, off=off: (bi, ti, off + hi))
    cw = lambda off: pl.BlockSpec((GDN_CONV, width), lambda bi, hi, ti, off=off: (0, off + hi))
    per_head = pl.BlockSpec((group, 1, LANES), lambda bi, hi, ti: (hi, 0, 0))
    alog_b = jnp.broadcast_to(a_log.astype(F32)[:, None, None], (nh, 1, LANES))
    dtb_b = jnp.broadcast_to(dt_bias.astype(F32)[:, None, None], (nh, 1, LANES))
    return pl.pallas_call(
        functools.partial(_gdn_kernel, tb=tb, n_heads=nh, group=group),
        grid=(b, nh // group, t // tb),
        in_specs=[col(0), col(hb), col(2 * hb), col(3 * hb),
                  pl.BlockSpec((1, tb, LANES), lambda bi, hi, ti: (bi, ti, 4 * d_delta // LANES)),
                  cw(0), cw(hb), cw(2 * hb), per_head, per_head,
                  pl.BlockSpec((1, LANES), lambda bi, hi, ti: (0, 0))],
        out_specs=pl.BlockSpec((1, tb, width), lambda bi, hi, ti: (bi, ti, hi)),
        out_shape=jax.ShapeDtypeStruct((b, t, d_delta), BF16),
        scratch_shapes=[pltpu.VMEM((3, HALO + tb, width), F32),
                        pltpu.VMEM((group, GDN_HEAD_DIM, GDN_HEAD_DIM), F32)],
        compiler_params=pltpu.CompilerParams(
            dimension_semantics=("parallel", "parallel", "arbitrary"), vmem_limit_bytes=VMEM_LIMIT),
        name="gdn_chunk",
    )(h3, h3, h3, h3, h3, conv_w, conv_w, conv_w, alog_b, dtb_b, norm_w.reshape(1, LANES))


def _rwkv_kernel(r_ref, k_ref, v_ref, xl_ref, mur_ref, muk_ref, muv_ref, mul_ref,
                 w0_ref, a0_ref, kk_ref, ka_ref, rk_ref, lg_ref, lb_ref,
                 w2_ref, a2_ref, g2_ref, o_ref, ext_ref, extl_ref, s_ref, *, tb, group):
    t_idx = pl.program_id(2)
    nc = tb // CHUNK
    c2 = 2 * CHUNK
    width = group * LANES

    @pl.when(t_idx == 0)
    def _():
        ext_ref[:, 0:HALO, :] = jnp.zeros((3, HALO, width), F32)
        extl_ref[0:HALO, :] = jnp.zeros((HALO, LORA_PAD), F32)
        s_ref[...] = jnp.zeros_like(s_ref)

    def shift(ext, x, mu):
        ext[HALO:HALO + tb, :] = x
        prev = ext[HALO - 1:HALO - 1 + tb, :]
        ext[0:HALO, :] = ext[tb:tb + HALO, :]
        return x + mu * (prev - x)

    r = shift(ext_ref.at[0], r_ref[0], mur_ref[...])
    k = shift(ext_ref.at[1], k_ref[0], muk_ref[...])
    v = shift(ext_ref.at[2], v_ref[0], muv_ref[...])
    xl = shift(extl_ref, xl_ref[0], mul_ref[...])

    w_log = -_softplus(-(w0_ref[...] + _dot(jnp.tanh(xl), w2_ref[0]))) - 0.5
    ld = -jnp.exp(w_log)
    a = _sigmoid(a0_ref[...] + _dot(xl, a2_ref[0]))
    gate = _dot(_sigmoid(xl), g2_ref[0])

    head0 = lax.broadcasted_iota(jnp.int32, (tb, LANES), 1) < RWKV_HEAD_DIM

    def head_sum(x):
        parts = []
        for j in range(group):
            xj = _lane_slab(x, j)
            s0 = jnp.sum(jnp.where(head0, xj, 0.0), axis=-1, keepdims=True)
            s1 = jnp.sum(jnp.where(head0, 0.0, xj), axis=-1, keepdims=True)
            parts.append(jnp.where(head0, s0, s1))
        return jnp.concatenate(parts, axis=-1) if group > 1 else parts[0]

    kk = k * kk_ref[...]
    kk = kk * lax.rsqrt(head_sum(kk * kk) + 1e-6)
    k2 = k * (1.0 + (a - 1.0) * ka_ref[...])
    bb = kk * a

    lc = _chunk_cumsum(ld, CHUNK)
    c3 = lambda t: t.reshape(nc, CHUNK, t.shape[-1])
    lc3 = c3(lc)
    l_last = lc3[:, CHUNK - 1:CHUNK, :]
    e_neg = jnp.exp(-lc)
    a_t = c3(-kk * jnp.exp(lc - ld))
    b_t = c3(bb * e_neg)
    k_t = c3(k2 * e_neg)
    r_t = c3(r * jnp.exp(lc))
    e_tail = jnp.exp(l_last - lc3)
    b_h = c3(bb) * e_tail
    k_h = c3(k2) * e_tail
    w_c = jnp.exp(l_last)
    v3 = c3(v)

    head0_3 = lax.broadcasted_iota(jnp.int32, (nc, CHUNK, LANES), 2) < RWKV_HEAD_DIM
    stack = lambda x: jnp.concatenate([jnp.where(head0_3, x, 0.0), jnp.where(head0_3, 0.0, x)], axis=1)
    dup = lambda x: jnp.concatenate([x, x], axis=1)
    ri = lax.broadcasted_iota(jnp.int32, (c2, c2), 0)
    ci = lax.broadcasted_iota(jnp.int32, (c2, c2), 1)
    same = (ri >= CHUNK) == (ci >= CHUNK)
    strict = (same & (ri > ci))[None]
    incl = (same & (ri >= ci))[None]
    h0c = lax.broadcasted_iota(jnp.int32, (CHUNK, LANES), 1) < RWKV_HEAD_DIM
    bd = ((lax.broadcasted_iota(jnp.int32, (LANES, LANES), 0) >= RWKV_HEAD_DIM)
          == (lax.broadcasted_iota(jnp.int32, (LANES, LANES), 1) >= RWKV_HEAD_DIM))

    streams = []
    for j in range(group):
        sl = lambda x: _lane_slab(x, j)
        a_s, r_s = stack(sl(a_t)), stack(sl(r_t))
        vv2 = dup(sl(v3))
        ar_bk = _bmm_nt(jnp.concatenate([a_s, r_s], axis=1),
                        jnp.concatenate([dup(sl(b_t)), dup(sl(k_t))], axis=1))
        m_ab = jnp.where(strict, ar_bk[:, :c2, :c2], 0.0)
        m_ak = jnp.where(strict, ar_bk[:, :c2, c2:], 0.0)
        m_rb = jnp.where(incl, ar_bk[:, c2:, :c2], 0.0)
        m_rk = jnp.where(incl, ar_bk[:, c2:, c2:], 0.0)
        tinv = _neumann_inverse(m_ab, 5)
        mv = _bmm(jnp.concatenate([m_ak, m_rk], axis=1), vv2)
        pt = _bmm(tinv, jnp.concatenate([mv[:, :c2], a_s], axis=-1))
        w_col = [jnp.broadcast_to(sl(w_c)[c], (LANES, LANES)).T for c in range(nc)]
        streams.append(dict(
            p=pt[..., :LANES], tr=jnp.concatenate([pt[..., LANES:], sl(r_t)], axis=1), yv=mv[:, c2:],
            m_rb=m_rb, v=sl(v3), bk=jnp.concatenate([sl(b_h), sl(k_h)], axis=1), w_col=w_col,
            s=s_ref[j], outs=[]))

    for c in range(nc):
        for st in streams:
            s = st['s']
            tr_s = _dot(st['tr'][c], s)
            u_s = st['p'][c] + tr_s[:c2]
            u = jnp.where(h0c, u_s[:CHUNK], u_s[CHUNK:])
            y_s = _dot(st['m_rb'][c], jnp.concatenate([u, u], axis=0)) + st['yv'][c]
            st['outs'].append(tr_s[c2:] + jnp.where(h0c, y_s[:CHUNK], y_s[CHUNK:]))
            upd = _dot_tn(st['bk'][c], jnp.concatenate([u, st['v'][c]], axis=0))
            st['s'] = s * st['w_col'][c] + jnp.where(bd, upd, 0.0)

    for j, st in enumerate(streams):
        s_ref[j] = st['s']
    y = jnp.concatenate([jnp.concatenate(st['outs'], axis=0) for st in streams], axis=-1)

    inv_n = 1.0 / RWKV_HEAD_DIM
    yc = y - head_sum(y) * inv_n
    y = yc * lax.rsqrt(head_sum(yc * yc) * inv_n + RWKV_LNX_EPS)
    y = y * lg_ref[...] + lb_ref[...]
    y = y + head_sum(r * k2 * rk_ref[...]) * v
    o_ref[0] = (y * gate).astype(o_ref.dtype)


def _rwkv(h3, col0, d_rwkv, mu, w0, w2p, a0, a2p, g2p, k_k, k_a, r_k, lnx_g, lnx_b, layer, tb, group=STREAMS):
    b, t, _ = h3.shape
    tb = min(tb, t)
    group = min(group, d_rwkv // LANES)
    width = group * LANES
    nblk = d_rwkv // width
    cb = col0 // width
    lora_blk = (col0 + 3 * d_rwkv) // LORA_PAD
    col = lambda off: pl.BlockSpec((1, tb, width), lambda bi, hi, ti, off=off: (bi, ti, cb + off + hi))
    vec = lambda off: pl.BlockSpec((1, width), lambda bi, hi, ti, off=off: (0, off + hi))
    lw = pl.BlockSpec((1, LORA_PAD, width), lambda bi, hi, ti: (layer, 0, hi))
    row = lambda x: x.astype(F32).reshape(1, -1)
    mu2 = row(mu)
    return pl.pallas_call(
        functools.partial(_rwkv_kernel, tb=tb, group=group),
        grid=(b, nblk, t // tb),
        in_specs=[col(0), col(nblk), col(2 * nblk),
                  pl.BlockSpec((1, tb, LORA_PAD), lambda bi, hi, ti: (bi, ti, lora_blk)),
                  vec(0), vec(nblk), vec(2 * nblk),
                  pl.BlockSpec((1, LORA_PAD), lambda bi, hi, ti: (0, 3 * d_rwkv // LORA_PAD)),
                  vec(0), vec(0), vec(0), vec(0), vec(0), vec(0), vec(0),
                  lw, lw, lw],
        out_specs=pl.BlockSpec((1, tb, width), lambda bi, hi, ti: (bi, ti, hi)),
        out_shape=jax.ShapeDtypeStruct((b, t, d_rwkv), BF16),
        scratch_shapes=[pltpu.VMEM((3, HALO + tb, width), F32),
                        pltpu.VMEM((HALO + tb, LORA_PAD), F32),
                        pltpu.VMEM((group, LANES, LANES), F32)],
        compiler_params=pltpu.CompilerParams(
            dimension_semantics=("parallel", "parallel", "arbitrary"), vmem_limit_bytes=VMEM_LIMIT),
        name="rwkv_chunk",
    )(h3, h3, h3, h3, mu2, mu2, mu2, mu2, row(w0), row(a0), row(k_k), row(k_a), row(r_k),
      row(lnx_g), row(lnx_b), w2p, a2p, g2p)


def _ln_router_kernel(mix_ref, x_ref, g_ref, b_ref, wh_ref, wl_ref, br_ref, x1_ref, x1p_ref, rt_ref):
    x1 = _layer_norm(DN_ALPHA * x_ref[...] + mix_ref[...], g_ref[...], b_ref[...])
    x1_ref[...] = x1
    x1p_ref[...] = _pack_halves(x1)
    hi, lo = _split_bf16(x1)
    d = lambda a, w: jnp.dot(a, w, preferred_element_type=F32)
    logits = d(hi, wh_ref[...]) + d(lo, wh_ref[...]) + d(hi, wl_ref[...]) + br_ref[...]

    lane_i = lax.broadcasted_iota(jnp.int32, logits.shape, 1)
    lane = lane_i.astype(F32)
    big = float(LANES)
    neg = -jnp.inf
    gmask = lane_i < N_GROUPS
    lg = jnp.where(gmask, logits, neg)
    m = jnp.max(lg, axis=-1, keepdims=True)
    p_grp = 1.0 / jnp.sum(jnp.where(gmask, jnp.exp(lg - m), 0.0), axis=-1, keepdims=True)
    g_sel = jnp.min(jnp.where(gmask & (lg == m), lane, big), axis=-1, keepdims=True)
    lane_grp = lax.shift_right_arithmetic(lane_i - N_GROUPS, EXPERTS_PER_GROUP.bit_length() - 1)
    emask = (lane_i >= N_GROUPS) & (lane_i < N_GROUPS + N_EXPERTS) & (lane_grp == g_sel.astype(jnp.int32))
    le = jnp.where(emask, logits, neg)
    v1 = jnp.max(le, axis=-1, keepdims=True)
    i1 = jnp.min(jnp.where(emask & (le == v1), lane, big), axis=-1, keepdims=True)
    emask2 = emask & (lane != i1)
    le2 = jnp.where(emask2, logits, neg)
    v2 = jnp.max(le2, axis=-1, keepdims=True)
    i2 = jnp.min(jnp.where(emask2 & (le2 == v2), lane, big), axis=-1, keepdims=True)
    t = jnp.exp(v2 - v1)
    den = 1.0 + t
    gate1 = p_grp / den
    gate2 = p_grp * t / den
    rt_ref[...] = jnp.where(lane_i == 0, i1 - N_GROUPS,
                            jnp.where(lane_i == 1, i2 - N_GROUPS,
                                      jnp.where(lane_i == 2, gate1, jnp.where(lane_i == 3, gate2, 0.0))))


def _ln_router(mix, x, g, b, wr_hi, wr_lo, br, tm=256):
    m, d = x.shape
    tm = min(tm, m)
    rows = lambda w: pl.BlockSpec((tm, w), lambda i: (i, 0))
    full = lambda s: pl.BlockSpec(s, lambda i: (0, 0))
    return pl.pallas_call(
        _ln_router_kernel,
        grid=(m // tm,),
        in_specs=[rows(d), rows(d), full((1, d)), full((1, d)),
                  full((d, LANES)), full((d, LANES)), full((1, LANES))],
        out_specs=[rows(d), rows(d // 2), rows(LANES)],
        out_shape=[jax.ShapeDtypeStruct((m, d), F32), jax.ShapeDtypeStruct((m, d // 2), jnp.uint32),
                   jax.ShapeDtypeStruct((m, LANES), F32)],
        compiler_params=pltpu.CompilerParams(
            dimension_semantics=("parallel",), vmem_limit_bytes=VMEM_LIMIT),
        name="ln_router",
    )(mix, x, g.reshape(1, d), b.reshape(1, d), wr_hi, wr_lo, br)


def _moe_kernel(te_ref, nt_ref, ts_ref, tl_ref, order_ref, x_hbm, wg_ref, wu_ref, wd_ref,
                y_hbm, xbuf, ybuf, sem_in, sem_out, *, tm):
    i = pl.program_id(0)
    n_tiles = nt_ref[0]
    slot = lax.rem(i, 2)
    n_tok = x_hbm.shape[0]
    n_pairs = y_hbm.shape[0] - tm

    def pairs_of(tile):
        start, length = ts_ref[tile], tl_ref[tile]
        for r in range(tm):
            valid = r < length
            yield r, valid, order_ref[jnp.where(valid, start + r, 0)]

    def gather(tile, sl):
        for r, valid, p in pairs_of(tile):
            tok = jnp.where(valid, jnp.where(p >= n_tok, p - n_tok, p), 0)
            pltpu.make_async_copy(x_hbm.at[pl.ds(tok, 1)], xbuf.at[sl, pl.ds(r, 1)], sem_in.at[sl]).start()

    def scatter_wait():
        pltpu.make_async_copy(ybuf, y_hbm.at[pl.ds(0, tm)], sem_out).wait()

    @pl.when(i == 0)
    def _():
        gather(0, 0)

    @pl.when(i + 1 < n_tiles)
    def _():
        gather(i + 1, 1 - slot)

    @pl.when(i < n_tiles)
    def _():
        pltpu.make_async_copy(x_hbm.at[pl.ds(0, tm)], xbuf.at[slot], sem_in.at[slot]).wait()
        xb = _unpack_halves(xbuf[slot]).astype(BF16)
        hg = jnp.dot(xb, wg_ref[0, 0].astype(BF16), preferred_element_type=F32)
        hu = jnp.dot(xb, wu_ref[0, 0].astype(BF16), preferred_element_type=F32)
        hact = hg * _sigmoid(hg) * hu
        y = jnp.dot(hact.astype(BF16), wd_ref[0, 0].astype(BF16), preferred_element_type=F32)

        @pl.when(i > 0)
        def _():
            scatter_wait()

        ybuf[...] = _pack_halves(y)

        for r, valid, p in pairs_of(i):
            dst = jnp.where(valid, p, n_pairs + r)
            pltpu.make_async_copy(ybuf.at[pl.ds(r, 1)], y_hbm.at[pl.ds(dst, 1)], sem_out).start()

        @pl.when(i == n_tiles - 1)
        def _():
            scatter_wait()
            fill = pltpu.make_async_copy(ybuf, y_hbm.at[pl.ds(y_hbm.shape[0] - tm, tm)], sem_out)
            fill.start()
            fill.wait()


def _moe(x1p, route, wg, wu, wd, layer, tm):
    m, dp = x1p.shape
    n_e, d, f = wg.shape[1], wg.shape[2], wg.shape[3]
    tm = min(tm, m)
    pairs = 2 * m
    nt_max = pairs // tm + n_e

    e_flat = jnp.concatenate([route[:, 0], route[:, 1]]).astype(jnp.int32)
    order = jnp.argsort(e_flat, stable=True).astype(jnp.int32)
    cnt = jnp.sum(e_flat[:, None] == jnp.arange(n_e, dtype=jnp.int32)[None, :], axis=0, dtype=jnp.int32)
    e_tiles = (cnt + tm - 1) // tm
    tile_end = jnp.cumsum(e_tiles)
    n_tiles = tile_end[-1:].astype(jnp.int32)
    tile_idx = jnp.arange(nt_max, dtype=jnp.int32)
    tile_e = jnp.minimum(jnp.searchsorted(tile_end, tile_idx, side='right'), n_e - 1).astype(jnp.int32)
    k_in_e = tile_idx - (tile_end - e_tiles)[tile_e]
    tile_start = ((jnp.cumsum(cnt) - cnt)[tile_e] + k_in_e * tm).astype(jnp.int32)
    tile_len = jnp.clip(cnt[tile_e] - k_in_e * tm, 0, tm).astype(jnp.int32)

    w_in_spec = pl.BlockSpec((1, 1, d, f), lambda i, te, nt, ts, tl, o: (layer, te[i], 0, 0))
    grid_spec = pltpu.PrefetchScalarGridSpec(
        num_scalar_prefetch=5,
        grid=(nt_max,),
        in_specs=[pl.BlockSpec(memory_space=pl.ANY), w_in_spec, w_in_spec,
                  pl.BlockSpec((1, 1, f, d), lambda i, te, nt, ts, tl, o: (layer, te[i], 0, 0))],
        out_specs=pl.BlockSpec(memory_space=pl.ANY),
        scratch_shapes=[pltpu.VMEM((2, tm, dp), jnp.uint32), pltpu.VMEM((tm, dp), jnp.uint32),
                        pltpu.SemaphoreType.DMA((2,)), pltpu.SemaphoreType.DMA],
    )
    return pl.pallas_call(
        functools.partial(_moe_kernel, tm=tm),
        grid_spec=grid_spec,
        out_shape=jax.ShapeDtypeStruct((pairs + tm, dp), jnp.uint32),
        compiler_params=pltpu.CompilerParams(
            dimension_semantics=("arbitrary",), vmem_limit_bytes=MOE_VMEM_LIMIT),
        name="moe_experts",
    )(tile_e, n_tiles, tile_start, tile_len, order, x1p, wg, wu, wd)


def _ple_ln_kernel(x1_ref, y0_ref, y1_ref, rt_ref, p_ref, wp_ref, gd_ref, gu_ref, g_ref, b_ref,
                   x2_ref, x2b_ref):
    d = lambda a, w: jnp.dot(a, w, preferred_element_type=F32)
    x1 = x1_ref[...]
    proj = d(p_ref[...].astype(BF16), wp_ref[0])
    gate = _sigmoid(d(d(x1.astype(BF16), gd_ref[0]).astype(BF16), gu_ref[0]))
    rt = rt_ref[...]
    ffn = rt[:, 2:3] * _unpack_halves(y0_ref[...]) + rt[:, 3:4] * _unpack_halves(y1_ref[...])
    y = DN_ALPHA * x1 + ffn + proj * gate
    x2 = _layer_norm(y, g_ref[...], b_ref[...])
    x2_ref[...] = x2
    x2b_ref[...] = x2.astype(BF16)


def _ple_ln(x1, y, route, p, wp, gd, gu, g, b, layer, tm=128):
    m, d = x1.shape
    pd = p.shape[1]
    tm = min(tm, m)
    nb = m // tm
    rows = lambda w: pl.BlockSpec((tm, w), lambda i: (i, 0))
    full = lambda s: pl.BlockSpec(s, lambda i: (0, 0))
    stack = lambda r, c: pl.BlockSpec((1, r, c), lambda i: (layer, 0, 0))
    return pl.pallas_call(
        _ple_ln_kernel,
        grid=(nb,),
        in_specs=[rows(d), rows(d // 2), pl.BlockSpec((tm, d // 2), lambda i: (i + nb, 0)), rows(LANES),
                  rows(pd), stack(pd, d), stack(d, pd), stack(pd, d), full((1, d)), full((1, d))],
        out_specs=[rows(d), rows(d)],
        out_shape=[jax.ShapeDtypeStruct((m, d), F32), jax.ShapeDtypeStruct((m, d), BF16)],
        compiler_params=pltpu.CompilerParams(
            dimension_semantics=("parallel",), vmem_limit_bytes=VMEM_LIMIT),
        name="ple_ln",
    )(x1, y, y, route, p, wp, gd, gu, g.reshape(1, d), b.reshape(1, d))


def _pad_rows(w, rows, offset):
    return jnp.pad(w, ((0, 0), (offset, rows - offset - w.shape[1]), (0, 0))).astype(BF16)


def _layer(layer, x, xb, p, wt, conv_w, a_log, dt_bias, norm_w, mu, w0, w2p, a0, a2p, g2p,
           k_k, k_a, r_k, lnx_g, lnx_b, w_out, ln1_g, ln1_b, w_grp, b_grp, w_rt, b_rt, wg, wu, wd,
           wp, gd, gu, ln2_g, ln2_b, *, batch, tb_gdn=1024, tb_rwkv=512):
    m, d_model = x.shape
    t = m // batch
    d_mix = w_out.shape[1]
    d_delta = d_mix // 2
    d_rwkv = d_mix - d_delta

    n_ab = 2 * (d_delta // GDN_HEAD_DIM)
    n_gdn, n_rwkv = 4 * d_delta + AB_PAD, wt.shape[1] - 4 * d_delta - n_ab
    h_gdn = _matmul_nt(xb, wt, layer, 0, n_gdn, 1024, _pick_tile(n_gdn, (768, 512, 256)))
    h_rwkv = _matmul_nt(xb, wt, layer, 4 * d_delta + n_ab, n_rwkv, 512,
                        _pick_tile(n_rwkv, (1280, 896, 512, 256)))
    o_gdn = _gdn(h_gdn.reshape(batch, t, -1), conv_w, a_log, dt_bias, norm_w, d_delta, tb_gdn)
    o_rwkv = _rwkv(h_rwkv.reshape(batch, t, -1), 0, d_rwkv, mu, w0, w2p, a0, a2p, g2p, k_k, k_a, r_k,
                   lnx_g, lnx_b, layer, tb_rwkv)
    mix_in = jnp.concatenate([o_gdn, o_rwkv], axis=-1).reshape(m, d_mix)
    mix = _matmul(mix_in, w_out, layer)

    w_router = jnp.concatenate(
        [w_grp, w_rt, jnp.zeros((d_model, LANES - N_GROUPS - N_EXPERTS), F32)], axis=1).astype(F32)
    wr_hi, wr_lo = _split_bf16(w_router)
    b_router = jnp.concatenate(
        [b_grp, b_rt, jnp.zeros((LANES - N_GROUPS - N_EXPERTS,), F32)]).astype(F32).reshape(1, LANES)
    x1, x1p, route = _ln_router(mix, x, ln1_g, ln1_b, wr_hi, wr_lo, b_router)

    y = _moe(x1p, route, wg, wu, wd, layer, MOE_TILE)
    return _ple_ln(x1, y, route, p, wp, gd, gu, ln2_g, ln2_b, layer)


def kernel(x, p, w_in, gdn_conv_w, gdn_a_log, gdn_dt_bias, gdn_norm_w, rwkv_mu, rwkv_w0, rwkv_w2, rwkv_a0, rwkv_a2, rwkv_g2, rwkv_k_k, rwkv_k_a, rwkv_r_k, rwkv_lnx_g, rwkv_lnx_b, w_out, ln1_g, ln1_b, moe_w_grp, moe_b_grp, moe_w_rt, moe_b_rt, moe_w_gate, moe_w_up, moe_w_down, ple_w_proj, ple_w_gate_down, ple_w_gate_up, ln2_g, ln2_b):
    batch, t, d_model = x.shape
    m = batch * t
    xf = x.reshape(m, d_model).astype(F32)
    xb = xf.astype(BF16)

    n_layers = w_in.shape[0]
    wt = jnp.swapaxes(w_in, 1, 2).astype(BF16)
    w_out_b = w_out.astype(BF16)
    wp_b, gd_b, gu_b = (w.astype(BF16) for w in (ple_w_proj, ple_w_gate_down, ple_w_gate_up))
    lora_w, lora_a = rwkv_w2.shape[1], rwkv_a2.shape[1]
    w2p = _pad_rows(rwkv_w2, LORA_PAD, 0)
    a2p = _pad_rows(rwkv_a2, LORA_PAD, lora_w)
    g2p = _pad_rows(rwkv_g2, LORA_PAD, lora_w + lora_a)

    for i in range(n_layers):
        xf, xb = _layer(
            i, xf, xb, p[i].reshape(m, -1), wt, gdn_conv_w[i], gdn_a_log[i], gdn_dt_bias[i],
            gdn_norm_w[i], rwkv_mu[i], rwkv_w0[i], w2p, rwkv_a0[i], a2p, g2p,
            rwkv_k_k[i], rwkv_k_a[i], rwkv_r_k[i], rwkv_lnx_g[i], rwkv_lnx_b[i], w_out_b,
            ln1_g[i], ln1_b[i], moe_w_grp[i], moe_b_grp[i], moe_w_rt[i], moe_b_rt[i],
            moe_w_gate, moe_w_up, moe_w_down, wp_b, gd_b, gu_b, ln2_g[i], ln2_b[i], batch=batch)
    return xf.reshape(batch, t, d_model)
```

```python
import functools

import jax
import jax.numpy as jnp
from jax import lax
from jax.experimental import pallas as pl
from jax.experimental.pallas import tpu as pltpu

F32 = jnp.float32
BF16 = jnp.bfloat16

DEPTH = 4
DN_ALPHA = (2 * DEPTH) ** 0.25
CHUNK = 64
LANES = 128
GDN_HEAD_DIM = 128
GDN_CONV = 4
RWKV_HEAD_DIM = 64
RWKV_LNX_EPS = 64e-5
N_GROUPS = 4
EXPERTS_PER_GROUP = 8
N_EXPERTS = N_GROUPS * EXPERTS_PER_GROUP
LORA_PAD = 256
AB_PAD = 256
HALO = 8
MOE_TILE = 256
PROJ_TILE = 512
ROW_ALIGN = 8
STREAMS = 2
VMEM_LIMIT = 48 * 1024 * 1024
MOE_VMEM_LIMIT = 58 * 1024 * 1024


def _sigmoid(x):
    return 1.0 / (1.0 + jnp.exp(-x))


def _softplus(x):
    return jnp.maximum(x, 0.0) + jnp.log(1.0 + jnp.exp(-jnp.abs(x)))


def _split_bf16(x):
    hi = x.astype(BF16)
    lo = (x - hi.astype(F32)).astype(BF16)
    return hi, lo


def _bmm(a, b):
    return jnp.einsum('nik,nkj->nij', a.astype(BF16), b.astype(BF16), preferred_element_type=F32)


def _bmm_nt(a, b):
    return jnp.einsum('nik,njk->nij', a.astype(BF16), b.astype(BF16), preferred_element_type=F32)


def _dot(a, b):
    return jnp.dot(a.astype(BF16), b.astype(BF16), preferred_element_type=F32)


def _dot_tn(a, b):
    return lax.dot_general(a.astype(BF16), b.astype(BF16), (((0,), (0,)), ((), ())),
                           preferred_element_type=F32)


def _neumann_inverse(m, steps):
    n = m.shape[-1]
    eye = (lax.broadcasted_iota(jnp.int32, (n, n), 0) == lax.broadcasted_iota(jnp.int32, (n, n), 1))
    p = m + eye.astype(F32)[None]
    mp = _bmm(m, m)
    for _ in range(steps - 1):
        both = _bmm(jnp.concatenate([p, mp], axis=1), mp)
        p = p + both[:, :n]
        mp = both[:, n:]
    return p + _bmm(p, mp)


def _chunk_cumsum(x, chunk):
    row = lax.broadcasted_iota(jnp.int32, x.shape, 0) & (chunk - 1)
    s = 1
    while s < chunk:
        x = x + jnp.where(row >= s, pltpu.roll(x, s, 0), 0.0)
        s *= 2
    return x


def _layer_norm(y, g, b):
    yc = y - jnp.mean(y, axis=-1, keepdims=True)
    var = jnp.mean(yc * yc, axis=-1, keepdims=True)
    return yc * lax.rsqrt(var + 1e-5) * g + b


def _lane_slab(x, j):
    return x[..., j * LANES:(j + 1) * LANES]


def _pack_halves(x):
    w = x.shape[-1] // 2
    bits = lambda t: lax.bitcast_convert_type(t.astype(BF16).astype(F32), jnp.uint32)
    return bits(x[:, :w]) | (bits(x[:, w:]) >> 16)


def _unpack_halves(p):
    hi = lax.bitcast_convert_type(p & jnp.uint32(0xFFFF0000), F32)
    lo = lax.bitcast_convert_type(p << 16, F32)
    return jnp.concatenate([hi, lo], axis=-1)


def _mm_kernel(xa_ref, xb_ref, w_ref, o_ref):
    ka = xa_ref.shape[1]
    o_ref[...] = (jnp.dot(xa_ref[...], w_ref[0, :ka, :], preferred_element_type=F32)
                  + jnp.dot(xb_ref[...], w_ref[0, ka:, :], preferred_element_type=F32))


def _matmul(xa, xb, w, layer, tm=1024, tn=PROJ_TILE):
    m, ka = xa.shape
    kb = xb.shape[1]
    k, n = w.shape[1], w.shape[2]
    tm, tn = min(tm, m), min(tn, n)
    return pl.pallas_call(
        _mm_kernel,
        grid=(m // tm, n // tn),
        in_specs=[pl.BlockSpec((tm, ka), lambda i, j: (i, 0)),
                  pl.BlockSpec((tm, kb), lambda i, j: (i, 0)),
                  pl.BlockSpec((1, k, tn), lambda i, j: (layer, 0, j))],
        out_specs=pl.BlockSpec((tm, tn), lambda i, j: (i, j)),
        out_shape=jax.ShapeDtypeStruct((m, n), F32),
        compiler_params=pltpu.CompilerParams(
            dimension_semantics=("parallel", "parallel"), vmem_limit_bytes=VMEM_LIMIT),
        name="proj_matmul",
    )(xa, xb, w)


def _mm_nt_kernel(x_ref, wt_ref, o_ref):
    o_ref[...] = lax.dot_general(x_ref[...], wt_ref[0], (((1,), (1,)), ((), ())),
                                 preferred_element_type=F32)


def _pick_tile(n, preferred):
    return next((t for t in preferred if n % t == 0), n)


def _matmul_nt(x, wt, layer, row0, n, tm, tn):
    m, k = x.shape
    tm = min(tm, m)
    return pl.pallas_call(
        _mm_nt_kernel,
        grid=(m // tm, n // tn),
        in_specs=[pl.BlockSpec((tm, k), lambda i, j: (i, 0)),
                  pl.BlockSpec((pl.Element(1), pl.Element(tn), pl.Element(k)),
                               lambda i, j: (layer, pl.multiple_of(row0 + j * tn, ROW_ALIGN), 0))],
        out_specs=pl.BlockSpec((tm, tn), lambda i, j: (i, j)),
        out_shape=jax.ShapeDtypeStruct((m, n), F32),
        compiler_params=pltpu.CompilerParams(
            dimension_semantics=("parallel", "parallel"), vmem_limit_bytes=VMEM_LIMIT),
        name="proj_matmul_nt",
    )(x, wt)


def _gdn_kernel(q_ref, k_ref, v_ref, z_ref, ab_ref, cwq_ref, cwk_ref, cwv_ref,
                alog_ref, dtb_ref, nw_ref, o_ref, ext_ref, s_ref, *, tb, n_heads, group):
    hg = pl.program_id(1)
    t_idx = pl.program_id(2)
    nc = tb // CHUNK
    width = group * LANES

    @pl.when(t_idx == 0)
    def _():
        ext_ref[:, 0:HALO, :] = jnp.zeros((3, HALO, width), F32)
        s_ref[...] = jnp.zeros_like(s_ref)

    def conv_silu(idx, x_ref, cw_ref):
        ext_ref[idx, HALO:HALO + tb, :] = x_ref[0]
        cw = cw_ref[...]
        acc = ext_ref[idx, HALO - 3:HALO - 3 + tb, :] * cw[0:1]
        for j in range(1, GDN_CONV):
            acc = acc + ext_ref[idx, HALO - 3 + j:HALO - 3 + j + tb, :] * cw[j:j + 1]
        ext_ref[idx, 0:HALO, :] = ext_ref[idx, tb:tb + HALO, :]
        return acc * _sigmoid(acc)

    q_all = conv_silu(0, q_ref, cwq_ref)
    k_all = conv_silu(1, k_ref, cwk_ref)
    v_all = conv_silu(2, v_ref, cwv_ref)
    ab = ab_ref[0]
    lane = lax.broadcasted_iota(jnp.int32, ab.shape, 1)
    ri = lax.broadcasted_iota(jnp.int32, (CHUNK, CHUNK), 0)
    ci = lax.broadcasted_iota(jnp.int32, (CHUNK, CHUNK), 1)
    causal = (ri >= ci)[None]
    strict = (ri > ci)[None]
    lane3 = lax.broadcasted_iota(jnp.int32, (nc, CHUNK, LANES), 2)
    c3 = lambda t: t.reshape(nc, CHUNK, t.shape[-1])

    streams = []
    for j in range(group):
        h = hg * group + j
        q = _lane_slab(q_all, j)
        k = _lane_slab(k_all, j)
        q = q * lax.rsqrt(jnp.sum(q * q, axis=-1, keepdims=True) + 1e-6) * (GDN_HEAD_DIM ** -0.5)
        k = k * lax.rsqrt(jnp.sum(k * k, axis=-1, keepdims=True) + 1e-6)
        a_col = jnp.sum(jnp.where(lane == h, ab, 0.0), axis=-1, keepdims=True)
        b_col = jnp.sum(jnp.where(lane == h + n_heads, ab, 0.0), axis=-1, keepdims=True)
        g = -jnp.exp(alog_ref[j]) * _softplus(a_col + dtb_ref[j])
        beta = jnp.broadcast_to(_sigmoid(b_col), g.shape)
        g = _chunk_cumsum(g, CHUNK)
        q3, k3, v3, g3, beta3 = c3(q), c3(k), c3(_lane_slab(v_all, j)), c3(g), c3(beta)
        g_last = g3[:, CHUNK - 1:CHUNK, :]

        p1 = g3.astype(BF16).astype(F32)
        p2 = (g3 - p1).astype(BF16).astype(F32)
        p3 = (g3 - p1 - p2).astype(BF16).astype(F32)
        pick = jnp.where((lane3 == 0) | (lane3 == 3), p1, jnp.where((lane3 == 1) | (lane3 == 4), p2, p3))
        col_op = jnp.where(lane3 < 3, pick, jnp.where(lane3 < 6, 1.0, 0.0))
        row_op = jnp.where(lane3 < 3, 1.0, jnp.where(lane3 < 6, -pick, 0.0))
        gdiff = _bmm_nt(col_op, row_op)
        decay = jnp.where(causal, jnp.exp(jnp.where(causal, gdiff, 0.0)), 0.0)

        kb3 = k3 * beta3
        kq_k = _bmm_nt(jnp.concatenate([kb3, q3], axis=1), k3)
        a_low = jnp.where(strict, kq_k[:, :CHUNK] * decay, 0.0)
        tinv = _neumann_inverse(-a_low, 5)
        eg = jnp.exp(g3)
        uw = _bmm(tinv, jnp.concatenate([v3 * beta3, kb3 * eg], axis=-1))
        streams.append(dict(
            u=uw[..., :GDN_HEAD_DIM], wq=jnp.concatenate([uw[..., GDN_HEAD_DIM:], q3 * eg], axis=1),
            attn=kq_k[:, CHUNK:] * decay, kd=k3 * jnp.exp(g_last - g3), gl=jnp.exp(g_last),
            s=s_ref[j], outs=[]))

    for c in range(nc):
        for st in streams:
            s = st['s']
            wq_s = _dot(st['wq'][c], s)
            v_new = st['u'][c] - wq_s[:CHUNK]
            st['outs'].append(wq_s[CHUNK:] + _dot(st['attn'][c], v_new))
            st['s'] = s * st['gl'][c] + _dot_tn(st['kd'][c], v_new)

    for j, st in enumerate(streams):
        s_ref[j] = st['s']
        o = jnp.concatenate(st['outs'], axis=0)
        o = o * lax.rsqrt(jnp.mean(o * o, axis=-1, keepdims=True) + 1e-6) * nw_ref[...]
        z = _lane_slab(z_ref[0], j)
        o_ref[0, :, j * LANES:(j + 1) * LANES] = (o * (z * _sigmoid(z))).astype(o_ref.dtype)


def _gdn(h3, conv_w, a_log, dt_bias, norm_w, d_delta, tb, group=STREAMS):
    b, t, _ = h3.shape
    nh = d_delta // GDN_HEAD_DIM
    tb = min(tb, t)
    group = min(group, nh)
    width = group * LANES
    hb = d_delta // width
    col = lambda off: pl.BlockSpec((1, tb, width), lambda bi, hi, ti, off=off: (bi, ti, off + hi))
    cw = lambda off: pl.BlockSpec((GDN_CONV, width), lambda bi, hi, ti, off=off: (0, off + hi))
    per_head = pl.BlockSpec((group, 1, LANES), lambda bi, hi, ti: (hi, 0, 0))
    alog_b = jnp.broadcast_to(a_log.astype(F32)[:, None, None], (nh, 1, LANES))
    dtb_b = jnp.broadcast_to(dt_bias.astype(F32)[:, None, None], (nh, 1, LANES))
    return pl.pallas_call(
        functools.partial(_gdn_kernel, tb=tb, n_heads=nh, group=group),
        grid=(b, nh // group, t // tb),
        in_specs=[col(0), col(hb), col(2 * hb), col(3 * hb),
                  pl.BlockSpec((1, tb, LANES), lambda bi, hi, ti: (bi, ti, 4 * d_delta // LANES)),
                  cw(0), cw(hb), cw(2 * hb), per_head, per_head,
                  pl.BlockSpec((1, LANES), lambda bi, hi, ti: (0, 0))],
        out_specs=pl.BlockSpec((1, tb, width), lambda bi, hi, ti: (bi, ti, hi)),
        out_shape=jax.ShapeDtypeStruct((b, t, d_delta), BF16),
        scratch_shapes=[pltpu.VMEM((3, HALO + tb, width), F32),
                        pltpu.VMEM((group, GDN_HEAD_DIM, GDN_HEAD_DIM), F32)],
        compiler_params=pltpu.CompilerParams(
            dimension_semantics=("parallel", "parallel", "arbitrary"), vmem_limit_bytes=VMEM_LIMIT),
        name="gdn_chunk",
    )(h3, h3, h3, h3, h3, conv_w, conv_w, conv_w, alog_b, dtb_b, norm_w.reshape(1, LANES))


def _rwkv_kernel(r_ref, k_ref, v_ref, xl_ref, mur_ref, muk_ref, muv_ref, mul_ref,
                 w0_ref, a0_ref, kk_ref, ka_ref, rk_ref, lg_ref, lb_ref,
                 w2_ref, a2_ref, g2_ref, o_ref, ext_ref, extl_ref, s_ref, *, tb, group):
    t_idx = pl.program_id(2)
    nc = tb // CHUNK
    c2 = 2 * CHUNK
    width = group * LANES

    @pl.when(t_idx == 0)
    def _():
        ext_ref[:, 0:HALO, :] = jnp.zeros((3, HALO, width), F32)
        extl_ref[0:HALO, :] = jnp.zeros((HALO, LORA_PAD), F32)
        s_ref[...] = jnp.zeros_like(s_ref)

    def shift(ext, x, mu):
        ext[HALO:HALO + tb, :] = x
        prev = ext[HALO - 1:HALO - 1 + tb, :]
        ext[0:HALO, :] = ext[tb:tb + HALO, :]
        return x + mu * (prev - x)

    r = shift(ext_ref.at[0], r_ref[0], mur_ref[...])
    k = shift(ext_ref.at[1], k_ref[0], muk_ref[...])
    v = shift(ext_ref.at[2], v_ref[0], muv_ref[...])
    xl = shift(extl_ref, xl_ref[0], mul_ref[...])

    w_log = -_softplus(-(w0_ref[...] + _dot(jnp.tanh(xl), w2_ref[0]))) - 0.5
    ld = -jnp.exp(w_log)
    a = _sigmoid(a0_ref[...] + _dot(xl, a2_ref[0]))
    gate = _dot(_sigmoid(xl), g2_ref[0])

    head0 = lax.broadcasted_iota(jnp.int32, (tb, LANES), 1) < RWKV_HEAD_DIM

    def head_sum(x):
        parts = []
        for j in range(group):
            xj = _lane_slab(x, j)
            s0 = jnp.sum(jnp.where(head0, xj, 0.0), axis=-1, keepdims=True)
            s1 = jnp.sum(jnp.where(head0, 0.0, xj), axis=-1, keepdims=True)
            parts.append(jnp.where(head0, s0, s1))
        return jnp.concatenate(parts, axis=-1) if group > 1 else parts[0]

    kk = k * kk_ref[...]
    kk = kk * lax.rsqrt(head_sum(kk * kk) + 1e-6)
    k2 = k * (1.0 + (a - 1.0) * ka_ref[...])
    bb = kk * a

    lc = _chunk_cumsum(ld, CHUNK)
    c3 = lambda t: t.reshape(nc, CHUNK, t.shape[-1])
    lc3 = c3(lc)
    l_last = lc3[:, CHUNK - 1:CHUNK, :]
    e_neg = jnp.exp(-lc)
    a_t = c3(-kk * jnp.exp(lc - ld))
    b_t = c3(bb * e_neg)
    k_t = c3(k2 * e_neg)
    r_t = c3(r * jnp.exp(lc))
    e_tail = jnp.exp(l_last - lc3)
    b_h = c3(bb) * e_tail
    k_h = c3(k2) * e_tail
    w_c = jnp.exp(l_last)
    v3 = c3(v)

    head0_3 = lax.broadcasted_iota(jnp.int32, (nc, CHUNK, LANES), 2) < RWKV_HEAD_DIM
    stack = lambda x: jnp.concatenate([jnp.where(head0_3, x, 0.0), jnp.where(head0_3, 0.0, x)], axis=1)
    dup = lambda x: jnp.concatenate([x, x], axis=1)
    ri = lax.broadcasted_iota(jnp.int32, (c2, c2), 0)
    ci = lax.broadcasted_iota(jnp.int32, (c2, c2), 1)
    same = (ri >= CHUNK) == (ci >= CHUNK)
    strict = (same & (ri > ci))[None]
    incl = (same & (ri >= ci))[None]
    h0c = lax.broadcasted_iota(jnp.int32, (CHUNK, LANES), 1) < RWKV_HEAD_DIM
    bd = ((lax.broadcasted_iota(jnp.int32, (LANES, LANES), 0) >= RWKV_HEAD_DIM)
          == (lax.broadcasted_iota(jnp.int32, (LANES, LANES), 1) >= RWKV_HEAD_DIM))

    streams = []
    for j in range(group):
        sl = lambda x: _lane_slab(x, j)
        a_s, r_s = stack(sl(a_t)), stack(sl(r_t))
        vv2 = dup(sl(v3))
        ar_bk = _bmm_nt(jnp.concatenate([a_s, r_s], axis=1),
                        jnp.concatenate([dup(sl(b_t)), dup(sl(k_t))], axis=1))
        m_ab = jnp.where(strict, ar_bk[:, :c2, :c2], 0.0)
        m_ak = jnp.where(strict, ar_bk[:, :c2, c2:], 0.0)
        m_rb = jnp.where(incl, ar_bk[:, c2:, :c2], 0.0)
        m_rk = jnp.where(incl, ar_bk[:, c2:, c2:], 0.0)
        tinv = _neumann_inverse(m_ab, 5)
        mv = _bmm(jnp.concatenate([m_ak, m_rk], axis=1), vv2)
        pt = _bmm(tinv, jnp.concatenate([mv[:, :c2], a_s], axis=-1))
        w_col = [jnp.broadcast_to(sl(w_c)[c], (LANES, LANES)).T for c in range(nc)]
        streams.append(dict(
            p=pt[..., :LANES], tr=jnp.concatenate([pt[..., LANES:], sl(r_t)], axis=1), yv=mv[:, c2:],
            m_rb=m_rb, v=sl(v3), bk=jnp.concatenate([sl(b_h), sl(k_h)], axis=1), w_col=w_col,
            s=s_ref[j], outs=[]))

    for c in range(nc):
        for st in streams:
            s = st['s']
            tr_s = _dot(st['tr'][c], s)
            u_s = st['p'][c] + tr_s[:c2]
            u = jnp.where(h0c, u_s[:CHUNK], u_s[CHUNK:])
            y_s = _dot(st['m_rb'][c], jnp.concatenate([u, u], axis=0)) + st['yv'][c]
            st['outs'].append(tr_s[c2:] + jnp.where(h0c, y_s[:CHUNK], y_s[CHUNK:]))
            upd = _dot_tn(st['bk'][c], jnp.concatenate([u, st['v'][c]], axis=0))
            st['s'] = s * st['w_col'][c] + jnp.where(bd, upd, 0.0)

    for j, st in enumerate(streams):
        s_ref[j] = st['s']
    y = jnp.concatenate([jnp.concatenate(st['outs'], axis=0) for st in streams], axis=-1)

    inv_n = 1.0 / RWKV_HEAD_DIM
    yc = y - head_sum(y) * inv_n
    y = yc * lax.rsqrt(head_sum(yc * yc) * inv_n + RWKV_LNX_EPS)
    y = y * lg_ref[...] + lb_ref[...]
    y = y + head_sum(r * k2 * rk_ref[...]) * v
    o_ref[0] = (y * gate).astype(o_ref.dtype)


def _rwkv(h3, col0, d_rwkv, mu, w0, w2p, a0, a2p, g2p, k_k, k_a, r_k, lnx_g, lnx_b, layer, tb, group=STREAMS):
    b, t, _ = h3.shape
    tb = min(tb, t)
    group = min(group, d_rwkv // LANES)
    width = group * LANES
    nblk = d_rwkv // width
    cb = col0 // width
    lora_blk = (col0 + 3 * d_rwkv) // LORA_PAD
    col = lambda off: pl.BlockSpec((1, tb, width), lambda bi, hi, ti, off=off: (bi, ti, cb + off + hi))
    vec = lambda off: pl.BlockSpec((1, width), lambda bi, hi, ti, off=off: (0, off + hi))
    lw = pl.BlockSpec((1, LORA_PAD, width), lambda bi, hi, ti: (layer, 0, hi))
    row = lambda x: x.astype(F32).reshape(1, -1)
    mu2 = row(mu)
    return pl.pallas_call(
        functools.partial(_rwkv_kernel, tb=tb, group=group),
        grid=(b, nblk, t // tb),
        in_specs=[col(0), col(nblk), col(2 * nblk),
                  pl.BlockSpec((1, tb, LORA_PAD), lambda bi, hi, ti: (bi, ti, lora_blk)),
                  vec(0), vec(nblk), vec(2 * nblk),
                  pl.BlockSpec((1, LORA_PAD), lambda bi, hi, ti: (0, 3 * d_rwkv // LORA_PAD)),
                  vec(0), vec(0), vec(0), vec(0), vec(0), vec(0), vec(0),
                  lw, lw, lw],
        out_specs=pl.BlockSpec((1, tb, width), lambda bi, hi, ti: (bi, ti, hi)),
        out_shape=jax.ShapeDtypeStruct((b, t, d_rwkv), BF16),
        scratch_shapes=[pltpu.VMEM((3, HALO + tb, width), F32),
                        pltpu.VMEM((HALO + tb, LORA_PAD), F32),
                        pltpu.VMEM((group, LANES, LANES), F32)],
        compiler_params=pltpu.CompilerParams(
            dimension_semantics=("parallel", "parallel", "arbitrary"), vmem_limit_bytes=VMEM_LIMIT),
        name="rwkv_chunk",
    )(h3, h3, h3, h3, mu2, mu2, mu2, mu2, row(w0), row(a0), row(k_k), row(k_a), row(r_k),
      row(lnx_g), row(lnx_b), w2p, a2p, g2p)


def _ln_router_kernel(mix_ref, x_ref, g_ref, b_ref, wh_ref, wl_ref, br_ref, x1_ref, x1p_ref, rt_ref):
    x1 = _layer_norm(DN_ALPHA * x_ref[...] + mix_ref[...], g_ref[...], b_ref[...])
    x1_ref[...] = x1
    x1p_ref[...] = _pack_halves(x1)
    hi, lo = _split_bf16(x1)
    d = lambda a, w: jnp.dot(a, w, preferred_element_type=F32)
    logits = d(hi, wh_ref[...]) + d(lo, wh_ref[...]) + d(hi, wl_ref[...]) + br_ref[...]

    lane_i = lax.broadcasted_iota(jnp.int32, logits.shape, 1)
    lane = lane_i.astype(F32)
    big = float(LANES)
    neg = -jnp.inf
    gmask = lane_i < N_GROUPS
    lg = jnp.where(gmask, logits, neg)
    m = jnp.max(lg, axis=-1, keepdims=True)
    p_grp = 1.0 / jnp.sum(jnp.where(gmask, jnp.exp(lg - m), 0.0), axis=-1, keepdims=True)
    g_sel = jnp.min(jnp.where(gmask & (lg == m), lane, big), axis=-1, keepdims=True)
    lane_grp = lax.shift_right_arithmetic(lane_i - N_GROUPS, EXPERTS_PER_GROUP.bit_length() - 1)
    emask = (lane_i >= N_GROUPS) & (lane_i < N_GROUPS + N_EXPERTS) & (lane_grp == g_sel.astype(jnp.int32))
    le = jnp.where(emask, logits, neg)
    v1 = jnp.max(le, axis=-1, keepdims=True)
    i1 = jnp.min(jnp.where(emask & (le == v1), lane, big), axis=-1, keepdims=True)
    emask2 = emask & (lane != i1)
    le2 = jnp.where(emask2, logits, neg)
    v2 = jnp.max(le2, axis=-1, keepdims=True)
    i2 = jnp.min(jnp.where(emask2 & (le2 == v2), lane, big), axis=-1, keepdims=True)
    t = jnp.exp(v2 - v1)
    den = 1.0 + t
    gate1 = p_grp / den
    gate2 = p_grp * t / den
    rt_ref[...] = jnp.where(lane_i == 0, i1 - N_GROUPS,
                            jnp.where(lane_i == 1, i2 - N_GROUPS,
                                      jnp.where(lane_i == 2, gate1, jnp.where(lane_i == 3, gate2, 0.0))))


def _ln_router(mix, x, g, b, wr_hi, wr_lo, br, tm=256):
    m, d = x.shape
    tm = min(tm, m)
    rows = lambda w: pl.BlockSpec((tm, w), lambda i: (i, 0))
    full = lambda s: pl.BlockSpec(s, lambda i: (0, 0))
    return pl.pallas_call(
        _ln_router_kernel,
        grid=(m // tm,),
        in_specs=[rows(d), rows(d), full((1, d)), full((1, d)),
                  full((d, LANES)), full((d, LANES)), full((1, LANES))],
        out_specs=[rows(d), rows(d // 2), rows(LANES)],
        out_shape=[jax.ShapeDtypeStruct((m, d), F32), jax.ShapeDtypeStruct((m, d // 2), jnp.uint32),
                   jax.ShapeDtypeStruct((m, LANES), F32)],
        compiler_params=pltpu.CompilerParams(
            dimension_semantics=("parallel",), vmem_limit_bytes=VMEM_LIMIT),
        name="ln_router",
    )(mix, x, g.reshape(1, d), b.reshape(1, d), wr_hi, wr_lo, br)


def _moe_kernel(te_ref, nt_ref, ts_ref, tl_ref, order_ref, x_hbm, wg_ref, wu_ref, wd_ref,
                y_hbm, xbuf, ybuf, wgb, wub, wdb, sem_in, sem_out, *, tm):
    i = pl.program_id(0)
    n_tiles = nt_ref[0]
    slot = lax.rem(i, 2)
    n_tok = x_hbm.shape[0]
    n_pairs = y_hbm.shape[0] - tm

    def pairs_of(tile):
        start, length = ts_ref[tile], tl_ref[tile]
        for r in range(tm):
            valid = r < length
            yield r, valid, order_ref[jnp.where(valid, start + r, 0)]

    def gather(tile, sl):
        for r, valid, p in pairs_of(tile):
            tok = jnp.where(valid, jnp.where(p >= n_tok, p - n_tok, p), 0)
            pltpu.make_async_copy(x_hbm.at[pl.ds(tok, 1)], xbuf.at[sl, pl.ds(r, 1)], sem_in.at[sl]).start()

    def scatter_wait():
        pltpu.make_async_copy(ybuf, y_hbm.at[pl.ds(0, tm)], sem_out).wait()

    @pl.when(i == 0)
    def _():
        gather(0, 0)

    @pl.when(i + 1 < n_tiles)
    def _():
        gather(i + 1, 1 - slot)

    @pl.when(i < n_tiles)
    def _():
        @pl.when((i == 0) | (te_ref[i] != te_ref[jnp.maximum(i - 1, 0)]))
        def _():
            wgb[...] = wg_ref[0, 0].astype(BF16)
            wub[...] = wu_ref[0, 0].astype(BF16)
            wdb[...] = wd_ref[0, 0].astype(BF16)

        pltpu.make_async_copy(x_hbm.at[pl.ds(0, tm)], xbuf.at[slot], sem_in.at[slot]).wait()
        xb = _unpack_halves(xbuf[slot]).astype(BF16)
        hg = jnp.dot(xb, wgb[...], preferred_element_type=F32)
        hu = jnp.dot(xb, wub[...], preferred_element_type=F32)
        hact = hg * _sigmoid(hg) * hu
        y = jnp.dot(hact.astype(BF16), wdb[...], preferred_element_type=F32)

        @pl.when(i > 0)
        def _():
            scatter_wait()

        ybuf[...] = _pack_halves(y)

        for r, valid, p in pairs_of(i):
            dst = jnp.where(valid, p, n_pairs + r)
            pltpu.make_async_copy(ybuf.at[pl.ds(r, 1)], y_hbm.at[pl.ds(dst, 1)], sem_out).start()

        @pl.when(i == n_tiles - 1)
        def _():
            scatter_wait()
            fill = pltpu.make_async_copy(ybuf, y_hbm.at[pl.ds(y_hbm.shape[0] - tm, tm)], sem_out)
            fill.start()
            fill.wait()


def _moe(x1p, route, wg, wu, wd, layer, tm):
    m, dp = x1p.shape
    n_e, d, f = wg.shape[1], wg.shape[2], wg.shape[3]
    tm = min(tm, m)
    pairs = 2 * m
    nt_max = pairs // tm + n_e

    e_flat = jnp.concatenate([route[:, 0], route[:, 1]]).astype(jnp.int32)
    order = jnp.argsort(e_flat, stable=True).astype(jnp.int32)
    cnt = jnp.sum(e_flat[:, None] == jnp.arange(n_e, dtype=jnp.int32)[None, :], axis=0, dtype=jnp.int32)
    e_tiles = (cnt + tm - 1) // tm
    tile_end = jnp.cumsum(e_tiles)
    n_tiles = tile_end[-1:].astype(jnp.int32)
    tile_idx = jnp.arange(nt_max, dtype=jnp.int32)
    tile_e = jnp.minimum(jnp.searchsorted(tile_end, tile_idx, side='right'), n_e - 1).astype(jnp.int32)
    k_in_e = tile_idx - (tile_end - e_tiles)[tile_e]
    tile_start = ((jnp.cumsum(cnt) - cnt)[tile_e] + k_in_e * tm).astype(jnp.int32)
    tile_len = jnp.clip(cnt[tile_e] - k_in_e * tm, 0, tm).astype(jnp.int32)

    w_in_spec = pl.BlockSpec((1, 1, d, f), lambda i, te, nt, ts, tl, o: (layer, te[i], 0, 0))
    grid_spec = pltpu.PrefetchScalarGridSpec(
        num_scalar_prefetch=5,
        grid=(nt_max,),
        in_specs=[pl.BlockSpec(memory_space=pl.ANY), w_in_spec, w_in_spec,
                  pl.BlockSpec((1, 1, f, d), lambda i, te, nt, ts, tl, o: (layer, te[i], 0, 0))],
        out_specs=pl.BlockSpec(memory_space=pl.ANY),
        scratch_shapes=[pltpu.VMEM((2, tm, dp), jnp.uint32), pltpu.VMEM((tm, dp), jnp.uint32),
                        pltpu.VMEM((d, f), BF16), pltpu.VMEM((d, f), BF16), pltpu.VMEM((f, d), BF16),
                        pltpu.SemaphoreType.DMA((2,)), pltpu.SemaphoreType.DMA],
    )
    return pl.pallas_call(
        functools.partial(_moe_kernel, tm=tm),
        grid_spec=grid_spec,
        out_shape=jax.ShapeDtypeStruct((pairs + tm, dp), jnp.uint32),
        compiler_params=pltpu.CompilerParams(
            dimension_semantics=("arbitrary",), vmem_limit_bytes=MOE_VMEM_LIMIT),
        name="moe_experts",
    )(tile_e, n_tiles, tile_start, tile_len, order, x1p, wg, wu, wd)


def _ple_ln_kernel(x1_ref, y0_ref, y1_ref, rt_ref, p_ref, wp_ref, gd_ref, gu_ref, g_ref, b_ref,
                   x2_ref, x2b_ref):
    d = lambda a, w: jnp.dot(a, w, preferred_element_type=F32)
    x1 = x1_ref[...]
    proj = d(p_ref[...].astype(BF16), wp_ref[0])
    gate = _sigmoid(d(d(x1.astype(BF16), gd_ref[0]).astype(BF16), gu_ref[0]))
    rt = rt_ref[...]
    ffn = rt[:, 2:3] * _unpack_halves(y0_ref[...]) + rt[:, 3:4] * _unpack_halves(y1_ref[...])
    y = DN_ALPHA * x1 + ffn + proj * gate
    x2 = _layer_norm(y, g_ref[...], b_ref[...])
    x2_ref[...] = x2
    x2b_ref[...] = x2.astype(BF16)


def _ple_ln(x1, y, route, p, wp, gd, gu, g, b, layer, tm=128):
    m, d = x1.shape
    pd = p.shape[1]
    tm = min(tm, m)
    nb = m // tm
    rows = lambda w: pl.BlockSpec((tm, w), lambda i: (i, 0))
    full = lambda s: pl.BlockSpec(s, lambda i: (0, 0))
    stack = lambda r, c: pl.BlockSpec((1, r, c), lambda i: (layer, 0, 0))
    return pl.pallas_call(
        _ple_ln_kernel,
        grid=(nb,),
        in_specs=[rows(d), rows(d // 2), pl.BlockSpec((tm, d // 2), lambda i: (i + nb, 0)), rows(LANES),
                  rows(pd), stack(pd, d), stack(d, pd), stack(pd, d), full((1, d)), full((1, d))],
        out_specs=[rows(d), rows(d)],
        out_shape=[jax.ShapeDtypeStruct((m, d), F32), jax.ShapeDtypeStruct((m, d), BF16)],
        compiler_params=pltpu.CompilerParams(
            dimension_semantics=("parallel",), vmem_limit_bytes=VMEM_LIMIT),
        name="ple_ln",
    )(x1, y, y, route, p, wp, gd, gu, g.reshape(1, d), b.reshape(1, d))


def _pad_rows(w, rows, offset):
    return jnp.pad(w, ((0, 0), (offset, rows - offset - w.shape[1]), (0, 0))).astype(BF16)


def _layer(layer, x, xb, p, wt, conv_w, a_log, dt_bias, norm_w, mu, w0, w2p, a0, a2p, g2p,
           k_k, k_a, r_k, lnx_g, lnx_b, w_out, ln1_g, ln1_b, w_grp, b_grp, w_rt, b_rt, wg, wu, wd,
           wp, gd, gu, ln2_g, ln2_b, *, batch, tb_gdn=1024, tb_rwkv=1024):
    m, d_model = x.shape
    t = m // batch
    d_mix = w_out.shape[1]
    d_delta = d_mix // 2
    d_rwkv = d_mix - d_delta

    n_ab = 2 * (d_delta // GDN_HEAD_DIM)
    n_gdn, n_rwkv = 4 * d_delta + AB_PAD, wt.shape[1] - 4 * d_delta - n_ab
    h_gdn = _matmul_nt(xb, wt, layer, 0, n_gdn, 1024, _pick_tile(n_gdn, (768, 512, 256)))
    h_rwkv = _matmul_nt(xb, wt, layer, 4 * d_delta + n_ab, n_rwkv, 512,
                        _pick_tile(n_rwkv, (1280, 896, 512, 256)))
    o_gdn = _gdn(h_gdn.reshape(batch, t, -1), conv_w, a_log, dt_bias, norm_w, d_delta, tb_gdn)
    o_rwkv = _rwkv(h_rwkv.reshape(batch, t, -1), 0, d_rwkv, mu, w0, w2p, a0, a2p, g2p, k_k, k_a, r_k,
                   lnx_g, lnx_b, layer, tb_rwkv)
    mix = _matmul(o_gdn.reshape(m, d_delta), o_rwkv.reshape(m, d_rwkv), w_out, layer)

    w_router = jnp.concatenate(
        [w_grp, w_rt, jnp.zeros((d_model, LANES - N_GROUPS - N_EXPERTS), F32)], axis=1).astype(F32)
    wr_hi, wr_lo = _split_bf16(w_router)
    b_router = jnp.concatenate(
        [b_grp, b_rt, jnp.zeros((LANES - N_GROUPS - N_EXPERTS,), F32)]).astype(F32).reshape(1, LANES)
    x1, x1p, route = _ln_router(mix, x, ln1_g, ln1_b, wr_hi, wr_lo, b_router)

    y = _moe(x1p, route, wg, wu, wd, layer, MOE_TILE)
    return _ple_ln(x1, y, route, p, wp, gd, gu, ln2_g, ln2_b, layer)


def kernel(x, p, w_in, gdn_conv_w, gdn_a_log, gdn_dt_bias, gdn_norm_w, rwkv_mu, rwkv_w0, rwkv_w2, rwkv_a0, rwkv_a2, rwkv_g2, rwkv_k_k, rwkv_k_a, rwkv_r_k, rwkv_lnx_g, rwkv_lnx_b, w_out, ln1_g, ln1_b, moe_w_grp, moe_b_grp, moe_w_rt, moe_b_rt, moe_w_gate, moe_w_up, moe_w_down, ple_w_proj, ple_w_gate_down, ple_w_gate_up, ln2_g, ln2_b):
    batch, t, d_model = x.shape
    m = batch * t
    xf = x.reshape(m, d_model).astype(F32)
    xb = xf.astype(BF16)

    n_layers = w_in.shape[0]
    wt = jnp.swapaxes(w_in, 1, 2).astype(BF16)
    w_out_b = w_out.astype(BF16)
    wp_b, gd_b, gu_b = (w.astype(BF16) for w in (ple_w_proj, ple_w_gate_down, ple_w_gate_up))
    lora_w, lora_a = rwkv_w2.shape[1], rwkv_a2.shape[1]
    w2p = _pad_rows(rwkv_w2, LORA_PAD, 0)
    a2p = _pad_rows(rwkv_a2, LORA_PAD, lora_w)
    g2p = _pad_rows(rwkv_g2, LORA_PAD, lora_w + lora_a)

    for i in range(n_layers):
        xf, xb = _layer(
            i, xf, xb, p[i].reshape(m, -1), wt, gdn_conv_w[i], gdn_a_log[i], gdn_dt_bias[i],
            gdn_norm_w[i], rwkv_mu[i], rwkv_w0[i], w2p, rwkv_a0[i], a2p, g2p,
            rwkv_k_k[i], rwkv_k_a[i], rwkv_r_k[i], rwkv_lnx_g[i], rwkv_lnx_b[i], w_out_b,
            ln1_g[i], ln1_b[i], moe_w_grp[i], moe_b_grp[i], moe_w_rt[i], moe_b_rt[i],
            moe_w_gate, moe_w_up, moe_w_down, wp_b, gd_b, gu_b, ln2_g[i], ln2_b[i], batch=batch)
    return xf.reshape(batch, t, d_model)
```

```python
import functools

import jax
import jax.numpy as jnp
from jax import lax
from jax.experimental import pallas as pl
from jax.experimental.pallas import tpu as pltpu

F32 = jnp.float32
BF16 = jnp.bfloat16

DEPTH = 4
DN_ALPHA = (2 * DEPTH) ** 0.25
CHUNK = 64
LANES = 128
GDN_HEAD_DIM = 128
GDN_CONV = 4
RWKV_HEAD_DIM = 64
RWKV_LNX_EPS = 64e-5
N_GROUPS = 4
EXPERTS_PER_GROUP = 8
N_EXPERTS = N_GROUPS * EXPERTS_PER_GROUP
LORA_PAD = 256
AB_PAD = 256
HALO = 8
MOE_TILE = 256
PROJ_TILE = 512
ROW_ALIGN = 8
STREAMS = 2
VMEM_LIMIT = 48 * 1024 * 1024
MOE_VMEM_LIMIT = 58 * 1024 * 1024


def _sigmoid(x):
    return 1.0 / (1.0 + jnp.exp(-x))


def _softplus(x):
    return jnp.maximum(x, 0.0) + jnp.log(1.0 + jnp.exp(-jnp.abs(x)))


def _split_bf16(x):
    hi = x.astype(BF16)
    lo = (x - hi.astype(F32)).astype(BF16)
    return hi, lo


def _bmm(a, b):
    return jnp.einsum('nik,nkj->nij', a.astype(BF16), b.astype(BF16), preferred_element_type=F32)


def _bmm_nt(a, b):
    return jnp.einsum('nik,njk->nij', a.astype(BF16), b.astype(BF16), preferred_element_type=F32)


def _dot(a, b):
    return jnp.dot(a.astype(BF16), b.astype(BF16), preferred_element_type=F32)


def _dot_tn(a, b):
    return lax.dot_general(a.astype(BF16), b.astype(BF16), (((0,), (0,)), ((), ())),
                           preferred_element_type=F32)


def _neumann_inverse(m, steps):
    n = m.shape[-1]
    eye = (lax.broadcasted_iota(jnp.int32, (n, n), 0) == lax.broadcasted_iota(jnp.int32, (n, n), 1))
    p = m + eye.astype(F32)[None]
    mp = _bmm(m, m)
    for _ in range(steps - 1):
        both = _bmm(jnp.concatenate([p, mp], axis=1), mp)
        p = p + both[:, :n]
        mp = both[:, n:]
    return p + _bmm(p, mp)


def _chunk_cumsum(x, chunk):
    row = lax.broadcasted_iota(jnp.int32, x.shape, 0) & (chunk - 1)
    s = 1
    while s < chunk:
        x = x + jnp.where(row >= s, pltpu.roll(x, s, 0), 0.0)
        s *= 2
    return x


def _shift_rows(cur, tail, s):
    rolled = pltpu.roll(cur, s, 0)
    row = lax.broadcasted_iota(jnp.int32, tail.shape, 0)
    head = jnp.where(row < s, pltpu.roll(tail, s, 0), rolled[:HALO])
    return jnp.concatenate([head, rolled[HALO:]], axis=0)


def _layer_norm(y, g, b):
    yc = y - jnp.mean(y, axis=-1, keepdims=True)
    var = jnp.mean(yc * yc, axis=-1, keepdims=True)
    return yc * lax.rsqrt(var + 1e-5) * g + b


def _lane_slab(x, j):
    return x[..., j * LANES:(j + 1) * LANES]


def _pack_halves(x):
    w = x.shape[-1] // 2
    bits = lambda t: lax.bitcast_convert_type(t.astype(BF16).astype(F32), jnp.uint32)
    return bits(x[:, :w]) | (bits(x[:, w:]) >> 16)


def _unpack_halves(p):
    hi = lax.bitcast_convert_type(p & jnp.uint32(0xFFFF0000), F32)
    lo = lax.bitcast_convert_type(p << 16, F32)
    return jnp.concatenate([hi, lo], axis=-1)


def _mm_kernel(xa_ref, xb_ref, w_ref, o_ref):
    ka = xa_ref.shape[1]
    o_ref[...] = (jnp.dot(xa_ref[...], w_ref[0, :ka, :], preferred_element_type=F32)
                  + jnp.dot(xb_ref[...], w_ref[0, ka:, :], preferred_element_type=F32))


def _matmul(xa, xb, w, layer, tm=1024, tn=PROJ_TILE):
    m, ka = xa.shape
    kb = xb.shape[1]
    k, n = w.shape[1], w.shape[2]
    tm, tn = min(tm, m), min(tn, n)
    return pl.pallas_call(
        _mm_kernel,
        grid=(m // tm, n // tn),
        in_specs=[pl.BlockSpec((tm, ka), lambda i, j: (i, 0)),
                  pl.BlockSpec((tm, kb), lambda i, j: (i, 0)),
                  pl.BlockSpec((1, k, tn), lambda i, j: (layer, 0, j))],
        out_specs=pl.BlockSpec((tm, tn), lambda i, j: (i, j)),
        out_shape=jax.ShapeDtypeStruct((m, n), F32),
        compiler_params=pltpu.CompilerParams(
            dimension_semantics=("parallel", "parallel"), vmem_limit_bytes=VMEM_LIMIT),
        name="proj_matmul",
    )(xa, xb, w)


def _mm_nt_kernel(x_ref, wt_ref, o_ref):
    o_ref[...] = lax.dot_general(x_ref[...], wt_ref[0], (((1,), (1,)), ((), ())),
                                 preferred_element_type=F32)


def _pick_tile(n, preferred):
    return next((t for t in preferred if n % t == 0), n)


def _matmul_nt(x, wt, layer, row0, n, tm, tn):
    m, k = x.shape
    tm = min(tm, m)
    return pl.pallas_call(
        _mm_nt_kernel,
        grid=(m // tm, n // tn),
        in_specs=[pl.BlockSpec((tm, k), lambda i, j: (i, 0)),
                  pl.BlockSpec((pl.Element(1), pl.Element(tn), pl.Element(k)),
                               lambda i, j: (layer, pl.multiple_of(row0 + j * tn, ROW_ALIGN), 0))],
        out_specs=pl.BlockSpec((tm, tn), lambda i, j: (i, j)),
        out_shape=jax.ShapeDtypeStruct((m, n), F32),
        compiler_params=pltpu.CompilerParams(
            dimension_semantics=("parallel", "parallel"), vmem_limit_bytes=VMEM_LIMIT),
        name="proj_matmul_nt",
    )(x, wt)


def _gdn_kernel(q_ref, k_ref, v_ref, z_ref, ab_ref, cwq_ref, cwk_ref, cwv_ref,
                alog_ref, dtb_ref, nw_ref, o_ref, ext_ref, s_ref, *, tb, n_heads, group):
    hg = pl.program_id(1)
    t_idx = pl.program_id(2)
    nc = tb // CHUNK
    width = group * LANES

    @pl.when(t_idx == 0)
    def _():
        ext_ref[...] = jnp.zeros_like(ext_ref)
        s_ref[...] = jnp.zeros_like(s_ref)

    def conv_silu(idx, x_ref, cw_ref):
        cur, tail, cw = x_ref[0], ext_ref[idx], cw_ref[...]
        acc = cur * cw[GDN_CONV - 1:GDN_CONV]
        for s in range(1, GDN_CONV):
            acc = acc + _shift_rows(cur, tail, s) * cw[GDN_CONV - 1 - s:GDN_CONV - s]
        ext_ref[idx] = cur[tb - HALO:]
        return acc * _sigmoid(acc)

    q_all = conv_silu(0, q_ref, cwq_ref)
    k_all = conv_silu(1, k_ref, cwk_ref)
    v_all = conv_silu(2, v_ref, cwv_ref)
    ab = ab_ref[0]
    lane = lax.broadcasted_iota(jnp.int32, ab.shape, 1)
    ri = lax.broadcasted_iota(jnp.int32, (CHUNK, CHUNK), 0)
    ci = lax.broadcasted_iota(jnp.int32, (CHUNK, CHUNK), 1)
    causal = (ri >= ci)[None]
    strict = (ri > ci)[None]
    lane3 = lax.broadcasted_iota(jnp.int32, (nc, CHUNK, LANES), 2)
    c3 = lambda t: t.reshape(nc, CHUNK, t.shape[-1])

    streams = []
    for j in range(group):
        h = hg * group + j
        q = _lane_slab(q_all, j)
        k = _lane_slab(k_all, j)
        q = q * lax.rsqrt(jnp.sum(q * q, axis=-1, keepdims=True) + 1e-6) * (GDN_HEAD_DIM ** -0.5)
        k = k * lax.rsqrt(jnp.sum(k * k, axis=-1, keepdims=True) + 1e-6)
        a_col = jnp.sum(jnp.where(lane == h, ab, 0.0), axis=-1, keepdims=True)
        b_col = jnp.sum(jnp.where(lane == h + n_heads, ab, 0.0), axis=-1, keepdims=True)
        g = -jnp.exp(alog_ref[j]) * _softplus(a_col + dtb_ref[j])
        beta = jnp.broadcast_to(_sigmoid(b_col), g.shape)
        g = _chunk_cumsum(g, CHUNK)
        q3, k3, v3, g3, beta3 = c3(q), c3(k), c3(_lane_slab(v_all, j)), c3(g), c3(beta)
        g_last = g3[:, CHUNK - 1:CHUNK, :]

        p1 = g3.astype(BF16).astype(F32)
        p2 = (g3 - p1).astype(BF16).astype(F32)
        p3 = (g3 - p1 - p2).astype(BF16).astype(F32)
        pick = jnp.where((lane3 == 0) | (lane3 == 3), p1, jnp.where((lane3 == 1) | (lane3 == 4), p2, p3))
        col_op = jnp.where(lane3 < 3, pick, jnp.where(lane3 < 6, 1.0, 0.0))
        row_op = jnp.where(lane3 < 3, 1.0, jnp.where(lane3 < 6, -pick, 0.0))
        gdiff = _bmm_nt(col_op, row_op)
        decay = jnp.where(causal, jnp.exp(jnp.where(causal, gdiff, 0.0)), 0.0)

        kb3 = k3 * beta3
        kq_k = _bmm_nt(jnp.concatenate([kb3, q3], axis=1), k3)
        a_low = jnp.where(strict, kq_k[:, :CHUNK] * decay, 0.0)
        tinv = _neumann_inverse(-a_low, 5)
        eg = jnp.exp(g3)
        uw = _bmm(tinv, jnp.concatenate([v3 * beta3, kb3 * eg], axis=-1))
        streams.append(dict(
            u=uw[..., :GDN_HEAD_DIM], wq=jnp.concatenate([uw[..., GDN_HEAD_DIM:], q3 * eg], axis=1),
            attn=kq_k[:, CHUNK:] * decay, kd=k3 * jnp.exp(g_last - g3), gl=jnp.exp(g_last),
            s=s_ref[j], outs=[]))

    for c in range(nc):
        for st in streams:
            s = st['s']
            wq_s = _dot(st['wq'][c], s)
            v_new = st['u'][c] - wq_s[:CHUNK]
            st['outs'].append(wq_s[CHUNK:] + _dot(st['attn'][c], v_new))
            st['s'] = s * st['gl'][c] + _dot_tn(st['kd'][c], v_new)

    for j, st in enumerate(streams):
        s_ref[j] = st['s']
        o = jnp.concatenate(st['outs'], axis=0)
        o = o * lax.rsqrt(jnp.mean(o * o, axis=-1, keepdims=True) + 1e-6) * nw_ref[...]
        z = _lane_slab(z_ref[0], j)
        o_ref[0, :, j * LANES:(j + 1) * LANES] = (o * (z * _sigmoid(z))).astype(o_ref.dtype)


def _gdn(h3, conv_w, a_log, dt_bias, norm_w, d_delta, tb, group=STREAMS):
    b, t, _ = h3.shape
    nh = d_delta // GDN_HEAD_DIM
    tb = min(tb, t)
    group = min(group, nh)
    width = group * LANES
    hb = d_delta // width
    col = lambda off: pl.BlockSpec((1, tb, width), lambda bi, hi, ti, off=off: (bi, ti, off + hi))
    cw = lambda off: pl.BlockSpec((GDN_CONV, width), lambda bi, hi, ti, off=off: (0, off + hi))
    per_head = pl.BlockSpec((group, 1, LANES), lambda bi, hi, ti: (hi, 0, 0))
    alog_b = jnp.broadcast_to(a_log.astype(F32)[:, None, None], (nh, 1, LANES))
    dtb_b = jnp.broadcast_to(dt_bias.astype(F32)[:, None, None], (nh, 1, LANES))
    return pl.pallas_call(
        functools.partial(_gdn_kernel, tb=tb, n_heads=nh, group=group),
        grid=(b, nh // group, t // tb),
        in_specs=[col(0), col(hb), col(2 * hb), col(3 * hb),
                  pl.BlockSpec((1, tb, LANES), lambda bi, hi, ti: (bi, ti, 4 * d_delta // LANES)),
                  cw(0), cw(hb), cw(2 * hb), per_head, per_head,
                  pl.BlockSpec((1, LANES), lambda bi, hi, ti: (0, 0))],
        out_specs=pl.BlockSpec((1, tb, width), lambda bi, hi, ti: (bi, ti, hi)),
        out_shape=jax.ShapeDtypeStruct((b, t, d_delta), BF16),
        scratch_shapes=[pltpu.VMEM((3, HALO, width), F32),
                        pltpu.VMEM((group, GDN_HEAD_DIM, GDN_HEAD_DIM), F32)],
        compiler_params=pltpu.CompilerParams(
            dimension_semantics=("parallel", "parallel", "arbitrary"), vmem_limit_bytes=VMEM_LIMIT),
        name="gdn_chunk",
    )(h3, h3, h3, h3, h3, conv_w, conv_w, conv_w, alog_b, dtb_b, norm_w.reshape(1, LANES))


def _rwkv_kernel(r_ref, k_ref, v_ref, xl_ref, mur_ref, muk_ref, muv_ref, mul_ref,
                 w0_ref, a0_ref, kk_ref, ka_ref, rk_ref, lg_ref, lb_ref,
                 w2_ref, a2_ref, g2_ref, o_ref, ext_ref, extl_ref, s_ref, *, tb, group):
    t_idx = pl.program_id(2)
    nc = tb // CHUNK
    c2 = 2 * CHUNK
    width = group * LANES

    @pl.when(t_idx == 0)
    def _():
        ext_ref[...] = jnp.zeros_like(ext_ref)
        extl_ref[...] = jnp.zeros_like(extl_ref)
        s_ref[...] = jnp.zeros_like(s_ref)

    def shift(ext, x, mu):
        prev = _shift_rows(x, ext[...], 1)
        ext[...] = x[tb - HALO:]
        return x + mu * (prev - x)

    r = shift(ext_ref.at[0], r_ref[0], mur_ref[...])
    k = shift(ext_ref.at[1], k_ref[0], muk_ref[...])
    v = shift(ext_ref.at[2], v_ref[0], muv_ref[...])
    xl = shift(extl_ref, xl_ref[0], mul_ref[...])

    w_log = -_softplus(-(w0_ref[...] + _dot(jnp.tanh(xl), w2_ref[0]))) - 0.5
    ld = -jnp.exp(w_log)
    a = _sigmoid(a0_ref[...] + _dot(xl, a2_ref[0]))
    gate = _dot(_sigmoid(xl), g2_ref[0])

    head0 = lax.broadcasted_iota(jnp.int32, (tb, LANES), 1) < RWKV_HEAD_DIM

    def head_sum(x):
        parts = []
        for j in range(group):
            xj = _lane_slab(x, j)
            s0 = jnp.sum(jnp.where(head0, xj, 0.0), axis=-1, keepdims=True)
            s1 = jnp.sum(jnp.where(head0, 0.0, xj), axis=-1, keepdims=True)
            parts.append(jnp.where(head0, s0, s1))
        return jnp.concatenate(parts, axis=-1) if group > 1 else parts[0]

    kk = k * kk_ref[...]
    kk = kk * lax.rsqrt(head_sum(kk * kk) + 1e-6)
    k2 = k * (1.0 + (a - 1.0) * ka_ref[...])
    bb = kk * a

    lc = _chunk_cumsum(ld, CHUNK)
    c3 = lambda t: t.reshape(nc, CHUNK, t.shape[-1])
    lc3 = c3(lc)
    l_last = lc3[:, CHUNK - 1:CHUNK, :]
    e_neg = jnp.exp(-lc)
    a_t = c3(-kk * jnp.exp(lc - ld))
    b_t = c3(bb * e_neg)
    k_t = c3(k2 * e_neg)
    r_t = c3(r * jnp.exp(lc))
    e_tail = jnp.exp(l_last - lc3)
    b_h = c3(bb) * e_tail
    k_h = c3(k2) * e_tail
    w_c = jnp.exp(l_last)
    v3 = c3(v)

    head0_3 = lax.broadcasted_iota(jnp.int32, (nc, CHUNK, LANES), 2) < RWKV_HEAD_DIM
    stack = lambda x: jnp.concatenate([jnp.where(head0_3, x, 0.0), jnp.where(head0_3, 0.0, x)], axis=1)
    dup = lambda x: jnp.concatenate([x, x], axis=1)
    ri = lax.broadcasted_iota(jnp.int32, (c2, c2), 0)
    ci = lax.broadcasted_iota(jnp.int32, (c2, c2), 1)
    same = (ri >= CHUNK) == (ci >= CHUNK)
    strict = (same & (ri > ci))[None]
    incl = (same & (ri >= ci))[None]
    h0c = lax.broadcasted_iota(jnp.int32, (CHUNK, LANES), 1) < RWKV_HEAD_DIM
    bd = ((lax.broadcasted_iota(jnp.int32, (LANES, LANES), 0) >= RWKV_HEAD_DIM)
          == (lax.broadcasted_iota(jnp.int32, (LANES, LANES), 1) >= RWKV_HEAD_DIM))

    streams = []
    for j in range(group):
        sl = lambda x: _lane_slab(x, j)
        a_s, r_s = stack(sl(a_t)), stack(sl(r_t))
        vv2 = dup(sl(v3))
        ar_bk = _bmm_nt(jnp.concatenate([a_s, r_s], axis=1),
                        jnp.concatenate([dup(sl(b_t)), dup(sl(k_t))], axis=1))
        m_ab = jnp.where(strict, ar_bk[:, :c2, :c2], 0.0)
        m_ak = jnp.where(strict, ar_bk[:, :c2, c2:], 0.0)
        m_rb = jnp.where(incl, ar_bk[:, c2:, :c2], 0.0)
        m_rk = jnp.where(incl, ar_bk[:, c2:, c2:], 0.0)
        tinv = _neumann_inverse(m_ab, 5)
        mv = _bmm(jnp.concatenate([m_ak, m_rk], axis=1), vv2)
        pt = _bmm(tinv, jnp.concatenate([mv[:, :c2], a_s], axis=-1))
        w_col = [jnp.broadcast_to(sl(w_c)[c], (LANES, LANES)).T for c in range(nc)]
        streams.append(dict(
            p=pt[..., :LANES], tr=jnp.concatenate([pt[..., LANES:], sl(r_t)], axis=1), yv=mv[:, c2:],
            m_rb=m_rb, v=sl(v3), bk=jnp.concatenate([sl(b_h), sl(k_h)], axis=1), w_col=w_col,
            s=s_ref[j], outs=[]))

    for c in range(nc):
        for st in streams:
            s = st['s']
            tr_s = _dot(st['tr'][c], s)
            u_s = st['p'][c] + tr_s[:c2]
            u = jnp.where(h0c, u_s[:CHUNK], u_s[CHUNK:])
            y_s = _dot(st['m_rb'][c], jnp.concatenate([u, u], axis=0)) + st['yv'][c]
            st['outs'].append(tr_s[c2:] + jnp.where(h0c, y_s[:CHUNK], y_s[CHUNK:]))
            upd = _dot_tn(st['bk'][c], jnp.concatenate([u, st['v'][c]], axis=0))
            st['s'] = s * st['w_col'][c] + jnp.where(bd, upd, 0.0)

    for j, st in enumerate(streams):
        s_ref[j] = st['s']
    y = jnp.concatenate([jnp.concatenate(st['outs'], axis=0) for st in streams], axis=-1)

    inv_n = 1.0 / RWKV_HEAD_DIM
    yc = y - head_sum(y) * inv_n
    y = yc * lax.rsqrt(head_sum(yc * yc) * inv_n + RWKV_LNX_EPS)
    y = y * lg_ref[...] + lb_ref[...]
    y = y + head_sum(r * k2 * rk_ref[...]) * v
    o_ref[0] = (y * gate).astype(o_ref.dtype)


def _rwkv(h3, col0, d_rwkv, mu, w0, w2p, a0, a2p, g2p, k_k, k_a, r_k, lnx_g, lnx_b, layer, tb, group=STREAMS):
    b, t, _ = h3.shape
    tb = min(tb, t)
    group = min(group, d_rwkv // LANES)
    width = group * LANES
    nblk = d_rwkv // width
    cb = col0 // width
    lora_blk = (col0 + 3 * d_rwkv) // LORA_PAD
    col = lambda off: pl.BlockSpec((1, tb, width), lambda bi, hi, ti, off=off: (bi, ti, cb + off + hi))
    vec = lambda off: pl.BlockSpec((1, width), lambda bi, hi, ti, off=off: (0, off + hi))
    lw = pl.BlockSpec((1, LORA_PAD, width), lambda bi, hi, ti: (layer, 0, hi))
    row = lambda x: x.astype(F32).reshape(1, -1)
    mu2 = row(mu)
    return pl.pallas_call(
        functools.partial(_rwkv_kernel, tb=tb, group=group),
        grid=(b, nblk, t // tb),
        in_specs=[col(0), col(nblk), col(2 * nblk),
                  pl.BlockSpec((1, tb, LORA_PAD), lambda bi, hi, ti: (bi, ti, lora_blk)),
                  vec(0), vec(nblk), vec(2 * nblk),
                  pl.BlockSpec((1, LORA_PAD), lambda bi, hi, ti: (0, 3 * d_rwkv // LORA_PAD)),
                  vec(0), vec(0), vec(0), vec(0), vec(0), vec(0), vec(0),
                  lw, lw, lw],
        out_specs=pl.BlockSpec((1, tb, width), lambda bi, hi, ti: (bi, ti, hi)),
        out_shape=jax.ShapeDtypeStruct((b, t, d_rwkv), BF16),
        scratch_shapes=[pltpu.VMEM((3, HALO, width), F32),
                        pltpu.VMEM((HALO, LORA_PAD), F32),
                        pltpu.VMEM((group, LANES, LANES), F32)],
        compiler_params=pltpu.CompilerParams(
            dimension_semantics=("parallel", "parallel", "arbitrary"), vmem_limit_bytes=VMEM_LIMIT),
        name="rwkv_chunk",
    )(h3, h3, h3, h3, mu2, mu2, mu2, mu2, row(w0), row(a0), row(k_k), row(k_a), row(r_k),
      row(lnx_g), row(lnx_b), w2p, a2p, g2p)


def _ln_router_kernel(mix_ref, x_ref, g_ref, b_ref, wh_ref, wl_ref, br_ref, x1_ref, x1p_ref, rt_ref):
    x1 = _layer_norm(DN_ALPHA * x_ref[...] + mix_ref[...], g_ref[...], b_ref[...])
    x1_ref[...] = x1
    x1p_ref[...] = _pack_halves(x1)
    hi, lo = _split_bf16(x1)
    d = lambda a, w: jnp.dot(a, w, preferred_element_type=F32)
    logits = d(hi, wh_ref[...]) + d(lo, wh_ref[...]) + d(hi, wl_ref[...]) + br_ref[...]

    lane_i = lax.broadcasted_iota(jnp.int32, logits.shape, 1)
    lane = lane_i.astype(F32)
    big = float(LANES)
    neg = -jnp.inf
    gmask = lane_i < N_GROUPS
    lg = jnp.where(gmask, logits, neg)
    m = jnp.max(lg, axis=-1, keepdims=True)
    p_grp = 1.0 / jnp.sum(jnp.where(gmask, jnp.exp(lg - m), 0.0), axis=-1, keepdims=True)
    g_sel = jnp.min(jnp.where(gmask & (lg == m), lane, big), axis=-1, keepdims=True)
    lane_grp = lax.shift_right_arithmetic(lane_i - N_GROUPS, EXPERTS_PER_GROUP.bit_length() - 1)
    emask = (lane_i >= N_GROUPS) & (lane_i < N_GROUPS + N_EXPERTS) & (lane_grp == g_sel.astype(jnp.int32))
    le = jnp.where(emask, logits, neg)
    v1 = jnp.max(le, axis=-1, keepdims=True)
    i1 = jnp.min(jnp.where(emask & (le == v1), lane, big), axis=-1, keepdims=True)
    emask2 = emask & (lane != i1)
    le2 = jnp.where(emask2, logits, neg)
    v2 = jnp.max(le2, axis=-1, keepdims=True)
    i2 = jnp.min(jnp.where(emask2 & (le2 == v2), lane, big), axis=-1, keepdims=True)
    t = jnp.exp(v2 - v1)
    den = 1.0 + t
    gate1 = p_grp / den
    gate2 = p_grp * t / den
    rt_ref[...] = jnp.where(lane_i == 0, i1 - N_GROUPS,
                            jnp.where(lane_i == 1, i2 - N_GROUPS,
                                      jnp.where(lane_i == 2, gate1, jnp.where(lane_i == 3, gate2, 0.0))))


def _ln_router(mix, x, g, b, wr_hi, wr_lo, br, tm=256):
    m, d = x.shape
    tm = min(tm, m)
    rows = lambda w: pl.BlockSpec((tm, w), lambda i: (i, 0))
    full = lambda s: pl.BlockSpec(s, lambda i: (0, 0))
    return pl.pallas_call(
        _ln_router_kernel,
        grid=(m // tm,),
        in_specs=[rows(d), rows(d), full((1, d)), full((1, d)),
                  full((d, LANES)), full((d, LANES)), full((1, LANES))],
        out_specs=[rows(d), rows(d // 2), rows(LANES)],
        out_shape=[jax.ShapeDtypeStruct((m, d), F32), jax.ShapeDtypeStruct((m, d // 2), jnp.uint32),
                   jax.ShapeDtypeStruct((m, LANES), F32)],
        compiler_params=pltpu.CompilerParams(
            dimension_semantics=("parallel",), vmem_limit_bytes=VMEM_LIMIT),
        name="ln_router",
    )(mix, x, g.reshape(1, d), b.reshape(1, d), wr_hi, wr_lo, br)


def _moe_kernel(te_ref, nt_ref, ts_ref, tl_ref, order_ref, x_hbm, wg_ref, wu_ref, wd_ref,
                y_hbm, xbuf, ybuf, wgb, wub, wdb, sem_in, sem_out, *, tm):
    i = pl.program_id(0)
    n_tiles = nt_ref[0]
    slot = lax.rem(i, 2)
    n_tok = x_hbm.shape[0]
    n_pairs = y_hbm.shape[0] - tm

    def pairs_of(tile):
        start, length = ts_ref[tile], tl_ref[tile]
        for r in range(tm):
            valid = r < length
            yield r, valid, order_ref[jnp.where(valid, start + r, 0)]

    def gather(tile, sl):
        for r, valid, p in pairs_of(tile):
            tok = jnp.where(valid, jnp.where(p >= n_tok, p - n_tok, p), 0)
            pltpu.make_async_copy(x_hbm.at[pl.ds(tok, 1)], xbuf.at[sl, pl.ds(r, 1)], sem_in.at[sl]).start()

    def scatter_wait():
        pltpu.make_async_copy(ybuf, y_hbm.at[pl.ds(0, tm)], sem_out).wait()

    @pl.when(i == 0)
    def _():
        gather(0, 0)

    @pl.when(i + 1 < n_tiles)
    def _():
        gather(i + 1, 1 - slot)

    @pl.when(i < n_tiles)
    def _():
        @pl.when((i == 0) | (te_ref[i] != te_ref[jnp.maximum(i - 1, 0)]))
        def _():
            wgb[...] = wg_ref[0, 0].astype(BF16)
            wub[...] = wu_ref[0, 0].astype(BF16)
            wdb[...] = wd_ref[0, 0].astype(BF16)

        pltpu.make_async_copy(x_hbm.at[pl.ds(0, tm)], xbuf.at[slot], sem_in.at[slot]).wait()
        xb = _unpack_halves(xbuf[slot]).astype(BF16)
        hg = jnp.dot(xb, wgb[...], preferred_element_type=F32)
        hu = jnp.dot(xb, wub[...], preferred_element_type=F32)
        hact = hg * _sigmoid(hg) * hu
        y = jnp.dot(hact.astype(BF16), wdb[...], preferred_element_type=F32)

        @pl.when(i > 0)
        def _():
            scatter_wait()

        ybuf[...] = _pack_halves(y)

        for r, valid, p in pairs_of(i):
            dst = jnp.where(valid, p, n_pairs + r)
            pltpu.make_async_copy(ybuf.at[pl.ds(r, 1)], y_hbm.at[pl.ds(dst, 1)], sem_out).start()

        @pl.when(i == n_tiles - 1)
        def _():
            scatter_wait()
            fill = pltpu.make_async_copy(ybuf, y_hbm.at[pl.ds(y_hbm.shape[0] - tm, tm)], sem_out)
            fill.start()
            fill.wait()


def _moe(x1p, route, wg, wu, wd, layer, tm):
    m, dp = x1p.shape
    n_e, d, f = wg.shape[1], wg.shape[2], wg.shape[3]
    tm = min(tm, m)
    pairs = 2 * m
    nt_max = pairs // tm + n_e

    e_flat = jnp.concatenate([route[:, 0], route[:, 1]]).astype(jnp.int32)
    order = jnp.argsort(e_flat, stable=True).astype(jnp.int32)
    cnt = jnp.sum(e_flat[:, None] == jnp.arange(n_e, dtype=jnp.int32)[None, :], axis=0, dtype=jnp.int32)
    e_tiles = (cnt + tm - 1) // tm
    tile_end = jnp.cumsum(e_tiles)
    n_tiles = tile_end[-1:].astype(jnp.int32)
    tile_idx = jnp.arange(nt_max, dtype=jnp.int32)
    tile_e = jnp.minimum(jnp.searchsorted(tile_end, tile_idx, side='right'), n_e - 1).astype(jnp.int32)
    k_in_e = tile_idx - (tile_end - e_tiles)[tile_e]
    tile_start = ((jnp.cumsum(cnt) - cnt)[tile_e] + k_in_e * tm).astype(jnp.int32)
    tile_len = jnp.clip(cnt[tile_e] - k_in_e * tm, 0, tm).astype(jnp.int32)

    w_in_spec = pl.BlockSpec((1, 1, d, f), lambda i, te, nt, ts, tl, o: (layer, te[i], 0, 0))
    grid_spec = pltpu.PrefetchScalarGridSpec(
        num_scalar_prefetch=5,
        grid=(nt_max,),
        in_specs=[pl.BlockSpec(memory_space=pl.ANY), w_in_spec, w_in_spec,
                  pl.BlockSpec((1, 1, f, d), lambda i, te, nt, ts, tl, o: (layer, te[i], 0, 0))],
        out_specs=pl.BlockSpec(memory_space=pl.ANY),
        scratch_shapes=[pltpu.VMEM((2, tm, dp), jnp.uint32), pltpu.VMEM((tm, dp), jnp.uint32),
                        pltpu.VMEM((d, f), BF16), pltpu.VMEM((d, f), BF16), pltpu.VMEM((f, d), BF16),
                        pltpu.SemaphoreType.DMA((2,)), pltpu.SemaphoreType.DMA],
    )
    return pl.pallas_call(
        functools.partial(_moe_kernel, tm=tm),
        grid_spec=grid_spec,
        out_shape=jax.ShapeDtypeStruct((pairs + tm, dp), jnp.uint32),
        compiler_params=pltpu.CompilerParams(
            dimension_semantics=("arbitrary",), vmem_limit_bytes=MOE_VMEM_LIMIT),
        name="moe_experts",
    )(tile_e, n_tiles, tile_start, tile_len, order, x1p, wg, wu, wd)


def _ple_ln_kernel(x1_ref, y0_ref, y1_ref, rt_ref, p_ref, wp_ref, gd_ref, gu_ref, g_ref, b_ref,
                   x2_ref, x2b_ref):
    d = lambda a, w: jnp.dot(a, w, preferred_element_type=F32)
    x1 = x1_ref[...]
    proj = d(p_ref[...].astype(BF16), wp_ref[0])
    gate = _sigmoid(d(d(x1.astype(BF16), gd_ref[0]).astype(BF16), gu_ref[0]))
    rt = rt_ref[...]
    ffn = rt[:, 2:3] * _unpack_halves(y0_ref[...]) + rt[:, 3:4] * _unpack_halves(y1_ref[...])
    y = DN_ALPHA * x1 + ffn + proj * gate
    x2 = _layer_norm(y, g_ref[...], b_ref[...])
    x2_ref[...] = x2
    x2b_ref[...] = x2.astype(BF16)


def _ple_ln(x1, y, route, p, wp, gd, gu, g, b, layer, tm=128):
    m, d = x1.shape
    pd = p.shape[1]
    tm = min(tm, m)
    nb = m // tm
    rows = lambda w: pl.BlockSpec((tm, w), lambda i: (i, 0))
    full = lambda s: pl.BlockSpec(s, lambda i: (0, 0))
    stack = lambda r, c: pl.BlockSpec((1, r, c), lambda i: (layer, 0, 0))
    return pl.pallas_call(
        _ple_ln_kernel,
        grid=(nb,),
        in_specs=[rows(d), rows(d // 2), pl.BlockSpec((tm, d // 2), lambda i: (i + nb, 0)), rows(LANES),
                  rows(pd), stack(pd, d), stack(d, pd), stack(pd, d), full((1, d)), full((1, d))],
        out_specs=[rows(d), rows(d)],
        out_shape=[jax.ShapeDtypeStruct((m, d), F32), jax.ShapeDtypeStruct((m, d), BF16)],
        compiler_params=pltpu.CompilerParams(
            dimension_semantics=("parallel",), vmem_limit_bytes=VMEM_LIMIT),
        name="ple_ln",
    )(x1, y, y, route, p, wp, gd, gu, g.reshape(1, d), b.reshape(1, d))


def _pad_rows(w, rows, offset):
    return jnp.pad(w, ((0, 0), (offset, rows - offset - w.shape[1]), (0, 0))).astype(BF16)


def _layer(layer, x, xb, p, wt, conv_w, a_log, dt_bias, norm_w, mu, w0, w2p, a0, a2p, g2p,
           k_k, k_a, r_k, lnx_g, lnx_b, w_out, ln1_g, ln1_b, w_grp, b_grp, w_rt, b_rt, wg, wu, wd,
           wp, gd, gu, ln2_g, ln2_b, *, batch, tb_gdn=1024, tb_rwkv=1024):
    m, d_model = x.shape
    t = m // batch
    d_mix = w_out.shape[1]
    d_delta = d_mix // 2
    d_rwkv = d_mix - d_delta

    n_ab = 2 * (d_delta // GDN_HEAD_DIM)
    n_gdn, n_rwkv = 4 * d_delta + AB_PAD, wt.shape[1] - 4 * d_delta - n_ab
    h_gdn = _matmul_nt(xb, wt, layer, 0, n_gdn, 1024, _pick_tile(n_gdn, (768, 512, 256)))
    h_rwkv = _matmul_nt(xb, wt, layer, 4 * d_delta + n_ab, n_rwkv, 512,
                        _pick_tile(n_rwkv, (1280, 896, 512, 256)))
    o_gdn = _gdn(h_gdn.reshape(batch, t, -1), conv_w, a_log, dt_bias, norm_w, d_delta, tb_gdn)
    o_rwkv = _rwkv(h_rwkv.reshape(batch, t, -1), 0, d_rwkv, mu, w0, w2p, a0, a2p, g2p, k_k, k_a, r_k,
                   lnx_g, lnx_b, layer, tb_rwkv)
    mix = _matmul(o_gdn.reshape(m, d_delta), o_rwkv.reshape(m, d_rwkv), w_out, layer)

    w_router = jnp.concatenate(
        [w_grp, w_rt, jnp.zeros((d_model, LANES - N_GROUPS - N_EXPERTS), F32)], axis=1).astype(F32)
    wr_hi, wr_lo = _split_bf16(w_router)
    b_router = jnp.concatenate(
        [b_grp, b_rt, jnp.zeros((LANES - N_GROUPS - N_EXPERTS,), F32)]).astype(F32).reshape(1, LANES)
    x1, x1p, route = _ln_router(mix, x, ln1_g, ln1_b, wr_hi, wr_lo, b_router)

    y = _moe(x1p, route, wg, wu, wd, layer, MOE_TILE)
    return _ple_ln(x1, y, route, p, wp, gd, gu, ln2_g, ln2_b, layer)


def kernel(x, p, w_in, gdn_conv_w, gdn_a_log, gdn_dt_bias, gdn_norm_w, rwkv_mu, rwkv_w0, rwkv_w2, rwkv_a0, rwkv_a2, rwkv_g2, rwkv_k_k, rwkv_k_a, rwkv_r_k, rwkv_lnx_g, rwkv_lnx_b, w_out, ln1_g, ln1_b, moe_w_grp, moe_b_grp, moe_w_rt, moe_b_rt, moe_w_gate, moe_w_up, moe_w_down, ple_w_proj, ple_w_gate_down, ple_w_gate_up, ln2_g, ln2_b):
    batch, t, d_model = x.shape
    m = batch * t
    xf = x.reshape(m, d_model).astype(F32)
    xb = xf.astype(BF16)

    n_layers = w_in.shape[0]
    wt = jnp.swapaxes(w_in, 1, 2).astype(BF16)
    w_out_b = w_out.astype(BF16)
    wp_b, gd_b, gu_b = (w.astype(BF16) for w in (ple_w_proj, ple_w_gate_down, ple_w_gate_up))
    lora_w, lora_a = rwkv_w2.shape[1], rwkv_a2.shape[1]
    w2p = _pad_rows(rwkv_w2, LORA_PAD, 0)
    a2p = _pad_rows(rwkv_a2, LORA_PAD, lora_w)
    g2p = _pad_rows(rwkv_g2, LORA_PAD, lora_w + lora_a)

    for i in range(n_layers):
        xf, xb = _layer(
            i, xf, xb, p[i].reshape(m, -1), wt, gdn_conv_w[i], gdn_a_log[i], gdn_dt_bias[i],
            gdn_norm_w[i], rwkv_mu[i], rwkv_w0[i], w2p, rwkv_a0[i], a2p, g2p,
            rwkv_k_k[i], rwkv_k_a[i], rwkv_r_k[i], rwkv_lnx_g[i], rwkv_lnx_b[i], w_out_b,
            ln1_g[i], ln1_b[i], moe_w_grp[i], moe_b_grp[i], moe_w_rt[i], moe_b_rt[i],
            moe_w_gate, moe_w_up, moe_w_down, wp_b, gd_b, gu_b, ln2_g[i], ln2_b[i], batch=batch)
    return xf.reshape(batch, t, d_model)
```

```python
import functools

import jax
import jax.numpy as jnp
from jax import lax
from jax.experimental import pallas as pl
from jax.experimental.pallas import tpu as pltpu

F32 = jnp.float32
BF16 = jnp.bfloat16

DEPTH = 4
DN_ALPHA = (2 * DEPTH) ** 0.25
CHUNK = 64
LANES = 128
GDN_HEAD_DIM = 128
GDN_CONV = 4
RWKV_HEAD_DIM = 64
RWKV_LNX_EPS = 64e-5
N_GROUPS = 4
EXPERTS_PER_GROUP = 8
N_EXPERTS = N_GROUPS * EXPERTS_PER_GROUP
LORA_PAD = 256
AB_PAD = 256
HALO = 8
MOE_TILE = 256
MOE_DMA_BLOCK = 32
PROJ_TILE = 512
ROW_ALIGN = 8
STREAMS = 2
VMEM_LIMIT = 48 * 1024 * 1024
MOE_VMEM_LIMIT = 58 * 1024 * 1024


def _sigmoid(x):
    return 1.0 / (1.0 + jnp.exp(-x))


def _softplus(x):
    return jnp.maximum(x, 0.0) + jnp.log(1.0 + jnp.exp(-jnp.abs(x)))


def _split_bf16(x):
    hi = x.astype(BF16)
    lo = (x - hi.astype(F32)).astype(BF16)
    return hi, lo


def _bmm(a, b):
    return jnp.einsum('nik,nkj->nij', a.astype(BF16), b.astype(BF16), preferred_element_type=F32)


def _bmm_nt(a, b):
    return jnp.einsum('nik,njk->nij', a.astype(BF16), b.astype(BF16), preferred_element_type=F32)


def _dot(a, b):
    return jnp.dot(a.astype(BF16), b.astype(BF16), preferred_element_type=F32)


def _dot_tn(a, b):
    return lax.dot_general(a.astype(BF16), b.astype(BF16), (((0,), (0,)), ((), ())),
                           preferred_element_type=F32)


def _neumann_inverse(m, steps):
    n = m.shape[-1]
    eye = (lax.broadcasted_iota(jnp.int32, (n, n), 0) == lax.broadcasted_iota(jnp.int32, (n, n), 1))
    p = m + eye.astype(F32)[None]
    mp = _bmm(m, m)
    for _ in range(steps - 1):
        both = _bmm(jnp.concatenate([p, mp], axis=1), mp)
        p = p + both[:, :n]
        mp = both[:, n:]
    return p + _bmm(p, mp)


def _chunk_cumsum(x, chunk):
    row = lax.broadcasted_iota(jnp.int32, x.shape, 0) & (chunk - 1)
    s = 1
    while s < chunk:
        x = x + jnp.where(row >= s, pltpu.roll(x, s, 0), 0.0)
        s *= 2
    return x


def _shift_rows(cur, tail, s):
    rolled = pltpu.roll(cur, s, 0)
    row = lax.broadcasted_iota(jnp.int32, tail.shape, 0)
    head = jnp.where(row < s, pltpu.roll(tail, s, 0), rolled[:HALO])
    return jnp.concatenate([head, rolled[HALO:]], axis=0)


def _layer_norm(y, g, b):
    yc = y - jnp.mean(y, axis=-1, keepdims=True)
    var = jnp.mean(yc * yc, axis=-1, keepdims=True)
    return yc * lax.rsqrt(var + 1e-5) * g + b


def _lane_slab(x, j):
    return x[..., j * LANES:(j + 1) * LANES]


def _pack_halves(x):
    w = x.shape[-1] // 2
    bits = lambda t: lax.bitcast_convert_type(t.astype(BF16).astype(F32), jnp.uint32)
    return bits(x[:, :w]) | (bits(x[:, w:]) >> 16)


def _unpack_halves(p):
    hi = lax.bitcast_convert_type(p & jnp.uint32(0xFFFF0000), F32)
    lo = lax.bitcast_convert_type(p << 16, F32)
    return jnp.concatenate([hi, lo], axis=-1)


def _mm_kernel(xa_ref, xb_ref, w_ref, o_ref):
    ka = xa_ref.shape[1]
    o_ref[...] = (jnp.dot(xa_ref[...], w_ref[0, :ka, :], preferred_element_type=F32)
                  + jnp.dot(xb_ref[...], w_ref[0, ka:, :], preferred_element_type=F32))


def _matmul(xa, xb, w, layer, tm=1024, tn=PROJ_TILE):
    m, ka = xa.shape
    kb = xb.shape[1]
    k, n = w.shape[1], w.shape[2]
    tm, tn = min(tm, m), min(tn, n)
    return pl.pallas_call(
        _mm_kernel,
        grid=(m // tm, n // tn),
        in_specs=[pl.BlockSpec((tm, ka), lambda i, j: (i, 0)),
                  pl.BlockSpec((tm, kb), lambda i, j: (i, 0)),
                  pl.BlockSpec((1, k, tn), lambda i, j: (layer, 0, j))],
        out_specs=pl.BlockSpec((tm, tn), lambda i, j: (i, j)),
        out_shape=jax.ShapeDtypeStruct((m, n), F32),
        compiler_params=pltpu.CompilerParams(
            dimension_semantics=("parallel", "parallel"), vmem_limit_bytes=VMEM_LIMIT),
        name="proj_matmul",
    )(xa, xb, w)


def _mm_nt_kernel(x_ref, wt_ref, o_ref):
    o_ref[...] = lax.dot_general(x_ref[...], wt_ref[0], (((1,), (1,)), ((), ())),
                                 preferred_element_type=F32)


def _pick_tile(n, preferred):
    return next((t for t in preferred if n % t == 0), n)


def _matmul_nt(x, wt, layer, row0, n, tm, tn):
    m, k = x.shape
    tm = min(tm, m)
    return pl.pallas_call(
        _mm_nt_kernel,
        grid=(m // tm, n // tn),
        in_specs=[pl.BlockSpec((tm, k), lambda i, j: (i, 0)),
                  pl.BlockSpec((pl.Element(1), pl.Element(tn), pl.Element(k)),
                               lambda i, j: (layer, pl.multiple_of(row0 + j * tn, ROW_ALIGN), 0))],
        out_specs=pl.BlockSpec((tm, tn), lambda i, j: (i, j)),
        out_shape=jax.ShapeDtypeStruct((m, n), F32),
        compiler_params=pltpu.CompilerParams(
            dimension_semantics=("parallel", "parallel"), vmem_limit_bytes=VMEM_LIMIT),
        name="proj_matmul_nt",
    )(x, wt)


def _gdn_kernel(q_ref, k_ref, v_ref, z_ref, ab_ref, cwq_ref, cwk_ref, cwv_ref,
                alog_ref, dtb_ref, nw_ref, o_ref, ext_ref, s_ref, *, tb, n_heads, group):
    hg = pl.program_id(1)
    t_idx = pl.program_id(2)
    nc = tb // CHUNK
    width = group * LANES

    @pl.when(t_idx == 0)
    def _():
        ext_ref[...] = jnp.zeros_like(ext_ref)
        s_ref[...] = jnp.zeros_like(s_ref)

    def conv_silu(idx, x_ref, cw_ref):
        cur, tail, cw = x_ref[0], ext_ref[idx], cw_ref[...]
        acc = cur * cw[GDN_CONV - 1:GDN_CONV]
        for s in range(1, GDN_CONV):
            acc = acc + _shift_rows(cur, tail, s) * cw[GDN_CONV - 1 - s:GDN_CONV - s]
        ext_ref[idx] = cur[tb - HALO:]
        return acc * _sigmoid(acc)

    q_all = conv_silu(0, q_ref, cwq_ref)
    k_all = conv_silu(1, k_ref, cwk_ref)
    v_all = conv_silu(2, v_ref, cwv_ref)
    ab = ab_ref[0]
    lane = lax.broadcasted_iota(jnp.int32, ab.shape, 1)
    ri = lax.broadcasted_iota(jnp.int32, (CHUNK, CHUNK), 0)
    ci = lax.broadcasted_iota(jnp.int32, (CHUNK, CHUNK), 1)
    causal = (ri >= ci)[None]
    strict = (ri > ci)[None]
    lane3 = lax.broadcasted_iota(jnp.int32, (nc, CHUNK, LANES), 2)
    c3 = lambda t: t.reshape(nc, CHUNK, t.shape[-1])

    streams = []
    for j in range(group):
        h = hg * group + j
        q = _lane_slab(q_all, j)
        k = _lane_slab(k_all, j)
        q = q * lax.rsqrt(jnp.sum(q * q, axis=-1, keepdims=True) + 1e-6) * (GDN_HEAD_DIM ** -0.5)
        k = k * lax.rsqrt(jnp.sum(k * k, axis=-1, keepdims=True) + 1e-6)
        a_col = jnp.sum(jnp.where(lane == h, ab, 0.0), axis=-1, keepdims=True)
        b_col = jnp.sum(jnp.where(lane == h + n_heads, ab, 0.0), axis=-1, keepdims=True)
        g = -jnp.exp(alog_ref[j]) * _softplus(a_col + dtb_ref[j])
        beta = jnp.broadcast_to(_sigmoid(b_col), g.shape)
        g = _chunk_cumsum(g, CHUNK)
        q3, k3, v3, g3, beta3 = c3(q), c3(k), c3(_lane_slab(v_all, j)), c3(g), c3(beta)
        g_last = g3[:, CHUNK - 1:CHUNK, :]

        p1 = g3.astype(BF16).astype(F32)
        p2 = (g3 - p1).astype(BF16).astype(F32)
        p3 = (g3 - p1 - p2).astype(BF16).astype(F32)
        pick = jnp.where((lane3 == 0) | (lane3 == 3), p1, jnp.where((lane3 == 1) | (lane3 == 4), p2, p3))
        col_op = jnp.where(lane3 < 3, pick, jnp.where(lane3 < 6, 1.0, 0.0))
        row_op = jnp.where(lane3 < 3, 1.0, jnp.where(lane3 < 6, -pick, 0.0))
        gdiff = _bmm_nt(col_op, row_op)
        decay = jnp.where(causal, jnp.exp(jnp.where(causal, gdiff, 0.0)), 0.0)

        kb3 = k3 * beta3
        kq_k = _bmm_nt(jnp.concatenate([kb3, q3], axis=1), k3)
        a_low = jnp.where(strict, kq_k[:, :CHUNK] * decay, 0.0)
        tinv = _neumann_inverse(-a_low, 5)
        eg = jnp.exp(g3)
        uw = _bmm(tinv, jnp.concatenate([v3 * beta3, kb3 * eg], axis=-1))
        streams.append(dict(
            u=uw[..., :GDN_HEAD_DIM], wq=jnp.concatenate([uw[..., GDN_HEAD_DIM:], q3 * eg], axis=1),
            attn=kq_k[:, CHUNK:] * decay, kd=k3 * jnp.exp(g_last - g3), gl=jnp.exp(g_last),
            s=s_ref[j], outs=[]))

    for c in range(nc):
        for st in streams:
            s = st['s']
            wq_s = _dot(st['wq'][c], s)
            v_new = st['u'][c] - wq_s[:CHUNK]
            st['outs'].append(wq_s[CHUNK:] + _dot(st['attn'][c], v_new))
            st['s'] = s * st['gl'][c] + _dot_tn(st['kd'][c], v_new)

    for j, st in enumerate(streams):
        s_ref[j] = st['s']
        o = jnp.concatenate(st['outs'], axis=0)
        o = o * lax.rsqrt(jnp.mean(o * o, axis=-1, keepdims=True) + 1e-6) * nw_ref[...]
        z = _lane_slab(z_ref[0], j)
        o_ref[0, :, j * LANES:(j + 1) * LANES] = (o * (z * _sigmoid(z))).astype(o_ref.dtype)


def _gdn(h3, conv_w, a_log, dt_bias, norm_w, d_delta, tb, group=STREAMS):
    b, t, _ = h3.shape
    nh = d_delta // GDN_HEAD_DIM
    tb = min(tb, t)
    group = min(group, nh)
    width = group * LANES
    hb = d_delta // width
    col = lambda off: pl.BlockSpec((1, tb, width), lambda bi, hi, ti, off=off: (bi, ti, off + hi))
    cw = lambda off: pl.BlockSpec((GDN_CONV, width), lambda bi, hi, ti, off=off: (0, off + hi))
    per_head = pl.BlockSpec((group, 1, LANES), lambda bi, hi, ti: (hi, 0, 0))
    alog_b = jnp.broadcast_to(a_log.astype(F32)[:, None, None], (nh, 1, LANES))
    dtb_b = jnp.broadcast_to(dt_bias.astype(F32)[:, None, None], (nh, 1, LANES))
    return pl.pallas_call(
        functools.partial(_gdn_kernel, tb=tb, n_heads=nh, group=group),
        grid=(b, nh // group, t // tb),
        in_specs=[col(0), col(hb), col(2 * hb), col(3 * hb),
                  pl.BlockSpec((1, tb, LANES), lambda bi, hi, ti: (bi, ti, 4 * d_delta // LANES)),
                  cw(0), cw(hb), cw(2 * hb), per_head, per_head,
                  pl.BlockSpec((1, LANES), lambda bi, hi, ti: (0, 0))],
        out_specs=pl.BlockSpec((1, tb, width), lambda bi, hi, ti: (bi, ti, hi)),
        out_shape=jax.ShapeDtypeStruct((b, t, d_delta), BF16),
        scratch_shapes=[pltpu.VMEM((3, HALO, width), F32),
                        pltpu.VMEM((group, GDN_HEAD_DIM, GDN_HEAD_DIM), F32)],
        compiler_params=pltpu.CompilerParams(
            dimension_semantics=("parallel", "parallel", "arbitrary"), vmem_limit_bytes=VMEM_LIMIT),
        name="gdn_chunk",
    )(h3, h3, h3, h3, h3, conv_w, conv_w, conv_w, alog_b, dtb_b, norm_w.reshape(1, LANES))


def _rwkv_kernel(r_ref, k_ref, v_ref, xl_ref, mur_ref, muk_ref, muv_ref, mul_ref,
                 w0_ref, a0_ref, kk_ref, ka_ref, rk_ref, lg_ref, lb_ref,
                 w2_ref, a2_ref, g2_ref, o_ref, ext_ref, extl_ref, s_ref, *, tb, group):
    t_idx = pl.program_id(2)
    nc = tb // CHUNK
    c2 = 2 * CHUNK
    width = group * LANES

    @pl.when(t_idx == 0)
    def _():
        ext_ref[...] = jnp.zeros_like(ext_ref)
        extl_ref[...] = jnp.zeros_like(extl_ref)
        s_ref[...] = jnp.zeros_like(s_ref)

    def shift(ext, x, mu):
        prev = _shift_rows(x, ext[...], 1)
        ext[...] = x[tb - HALO:]
        return x + mu * (prev - x)

    r = shift(ext_ref.at[0], r_ref[0], mur_ref[...])
    k = shift(ext_ref.at[1], k_ref[0], muk_ref[...])
    v = shift(ext_ref.at[2], v_ref[0], muv_ref[...])
    xl = shift(extl_ref, xl_ref[0], mul_ref[...])

    w_log = -_softplus(-(w0_ref[...] + _dot(jnp.tanh(xl), w2_ref[0]))) - 0.5
    ld = -jnp.exp(w_log)
    a = _sigmoid(a0_ref[...] + _dot(xl, a2_ref[0]))
    gate = _dot(_sigmoid(xl), g2_ref[0])

    head0 = lax.broadcasted_iota(jnp.int32, (tb, LANES), 1) < RWKV_HEAD_DIM

    def head_sum(x):
        parts = []
        for j in range(group):
            xj = _lane_slab(x, j)
            s0 = jnp.sum(jnp.where(head0, xj, 0.0), axis=-1, keepdims=True)
            s1 = jnp.sum(jnp.where(head0, 0.0, xj), axis=-1, keepdims=True)
            parts.append(jnp.where(head0, s0, s1))
        return jnp.concatenate(parts, axis=-1) if group > 1 else parts[0]

    kk = k * kk_ref[...]
    kk = kk * lax.rsqrt(head_sum(kk * kk) + 1e-6)
    k2 = k * (1.0 + (a - 1.0) * ka_ref[...])
    bb = kk * a

    lc = _chunk_cumsum(ld, CHUNK)
    c3 = lambda t: t.reshape(nc, CHUNK, t.shape[-1])
    lc3 = c3(lc)
    l_last = lc3[:, CHUNK - 1:CHUNK, :]
    e_neg = jnp.exp(-lc)
    a_t = c3(-kk * jnp.exp(lc - ld))
    b_t = c3(bb * e_neg)
    k_t = c3(k2 * e_neg)
    r_t = c3(r * jnp.exp(lc))
    e_tail = jnp.exp(l_last - lc3)
    b_h = c3(bb) * e_tail
    k_h = c3(k2) * e_tail
    w_c = jnp.exp(l_last)
    v3 = c3(v)

    head0_3 = lax.broadcasted_iota(jnp.int32, (nc, CHUNK, LANES), 2) < RWKV_HEAD_DIM
    stack = lambda x: jnp.concatenate([jnp.where(head0_3, x, 0.0), jnp.where(head0_3, 0.0, x)], axis=1)
    dup = lambda x: jnp.concatenate([x, x], axis=1)
    ri = lax.broadcasted_iota(jnp.int32, (c2, c2), 0)
    ci = lax.broadcasted_iota(jnp.int32, (c2, c2), 1)
    same = (ri >= CHUNK) == (ci >= CHUNK)
    strict = (same & (ri > ci))[None]
    incl = (same & (ri >= ci))[None]
    h0c = lax.broadcasted_iota(jnp.int32, (CHUNK, LANES), 1) < RWKV_HEAD_DIM
    bd = ((lax.broadcasted_iota(jnp.int32, (LANES, LANES), 0) >= RWKV_HEAD_DIM)
          == (lax.broadcasted_iota(jnp.int32, (LANES, LANES), 1) >= RWKV_HEAD_DIM))

    streams = []
    for j in range(group):
        sl = lambda x: _lane_slab(x, j)
        a_s, r_s = stack(sl(a_t)), stack(sl(r_t))
        vv2 = dup(sl(v3))
        ar_bk = _bmm_nt(jnp.concatenate([a_s, r_s], axis=1),
                        jnp.concatenate([dup(sl(b_t)), dup(sl(k_t))], axis=1))
        m_ab = jnp.where(strict, ar_bk[:, :c2, :c2], 0.0)
        m_ak = jnp.where(strict, ar_bk[:, :c2, c2:], 0.0)
        m_rb = jnp.where(incl, ar_bk[:, c2:, :c2], 0.0)
        m_rk = jnp.where(incl, ar_bk[:, c2:, c2:], 0.0)
        tinv = _neumann_inverse(m_ab, 5)
        mv = _bmm(jnp.concatenate([m_ak, m_rk], axis=1), vv2)
        pt = _bmm(tinv, jnp.concatenate([mv[:, :c2], a_s], axis=-1))
        w_col = [jnp.broadcast_to(sl(w_c)[c], (LANES, LANES)).T for c in range(nc)]
        streams.append(dict(
            p=pt[..., :LANES], tr=jnp.concatenate([pt[..., LANES:], sl(r_t)], axis=1), yv=mv[:, c2:],
            m_rb=m_rb, v=sl(v3), bk=jnp.concatenate([sl(b_h), sl(k_h)], axis=1), w_col=w_col,
            s=s_ref[j], outs=[]))

    for c in range(nc):
        for st in streams:
            s = st['s']
            tr_s = _dot(st['tr'][c], s)
            u_s = st['p'][c] + tr_s[:c2]
            u = jnp.where(h0c, u_s[:CHUNK], u_s[CHUNK:])
            y_s = _dot(st['m_rb'][c], jnp.concatenate([u, u], axis=0)) + st['yv'][c]
            st['outs'].append(tr_s[c2:] + jnp.where(h0c, y_s[:CHUNK], y_s[CHUNK:]))
            upd = _dot_tn(st['bk'][c], jnp.concatenate([u, st['v'][c]], axis=0))
            st['s'] = s * st['w_col'][c] + jnp.where(bd, upd, 0.0)

    for j, st in enumerate(streams):
        s_ref[j] = st['s']
    y = jnp.concatenate([jnp.concatenate(st['outs'], axis=0) for st in streams], axis=-1)

    inv_n = 1.0 / RWKV_HEAD_DIM
    yc = y - head_sum(y) * inv_n
    y = yc * lax.rsqrt(head_sum(yc * yc) * inv_n + RWKV_LNX_EPS)
    y = y * lg_ref[...] + lb_ref[...]
    y = y + head_sum(r * k2 * rk_ref[...]) * v
    o_ref[0] = (y * gate).astype(o_ref.dtype)


def _rwkv(h3, col0, d_rwkv, mu, w0, w2p, a0, a2p, g2p, k_k, k_a, r_k, lnx_g, lnx_b, layer, tb, group=STREAMS):
    b, t, _ = h3.shape
    tb = min(tb, t)
    group = min(group, d_rwkv // LANES)
    width = group * LANES
    nblk = d_rwkv // width
    cb = col0 // width
    lora_blk = (col0 + 3 * d_rwkv) // LORA_PAD
    col = lambda off: pl.BlockSpec((1, tb, width), lambda bi, hi, ti, off=off: (bi, ti, cb + off + hi))
    vec = lambda off: pl.BlockSpec((1, width), lambda bi, hi, ti, off=off: (0, off + hi))
    lw = pl.BlockSpec((1, LORA_PAD, width), lambda bi, hi, ti: (layer, 0, hi))
    row = lambda x: x.astype(F32).reshape(1, -1)
    mu2 = row(mu)
    return pl.pallas_call(
        functools.partial(_rwkv_kernel, tb=tb, group=group),
        grid=(b, nblk, t // tb),
        in_specs=[col(0), col(nblk), col(2 * nblk),
                  pl.BlockSpec((1, tb, LORA_PAD), lambda bi, hi, ti: (bi, ti, lora_blk)),
                  vec(0), vec(nblk), vec(2 * nblk),
                  pl.BlockSpec((1, LORA_PAD), lambda bi, hi, ti: (0, 3 * d_rwkv // LORA_PAD)),
                  vec(0), vec(0), vec(0), vec(0), vec(0), vec(0), vec(0),
                  lw, lw, lw],
        out_specs=pl.BlockSpec((1, tb, width), lambda bi, hi, ti: (bi, ti, hi)),
        out_shape=jax.ShapeDtypeStruct((b, t, d_rwkv), BF16),
        scratch_shapes=[pltpu.VMEM((3, HALO, width), F32),
                        pltpu.VMEM((HALO, LORA_PAD), F32),
                        pltpu.VMEM((group, LANES, LANES), F32)],
        compiler_params=pltpu.CompilerParams(
            dimension_semantics=("parallel", "parallel", "arbitrary"), vmem_limit_bytes=VMEM_LIMIT),
        name="rwkv_chunk",
    )(h3, h3, h3, h3, mu2, mu2, mu2, mu2, row(w0), row(a0), row(k_k), row(k_a), row(r_k),
      row(lnx_g), row(lnx_b), w2p, a2p, g2p)


def _ln_router_kernel(mix_ref, x_ref, g_ref, b_ref, wh_ref, wl_ref, br_ref, x1_ref, x1p_ref, rt_ref):
    x1 = _layer_norm(DN_ALPHA * x_ref[...] + mix_ref[...], g_ref[...], b_ref[...])
    x1_ref[...] = x1
    x1p_ref[...] = _pack_halves(x1)
    hi, lo = _split_bf16(x1)
    d = lambda a, w: jnp.dot(a, w, preferred_element_type=F32)
    logits = d(hi, wh_ref[...]) + d(lo, wh_ref[...]) + d(hi, wl_ref[...]) + br_ref[...]

    lane_i = lax.broadcasted_iota(jnp.int32, logits.shape, 1)
    lane = lane_i.astype(F32)
    big = float(LANES)
    neg = -jnp.inf
    gmask = lane_i < N_GROUPS
    lg = jnp.where(gmask, logits, neg)
    m = jnp.max(lg, axis=-1, keepdims=True)
    p_grp = 1.0 / jnp.sum(jnp.where(gmask, jnp.exp(lg - m), 0.0), axis=-1, keepdims=True)
    g_sel = jnp.min(jnp.where(gmask & (lg == m), lane, big), axis=-1, keepdims=True)
    lane_grp = lax.shift_right_arithmetic(lane_i - N_GROUPS, EXPERTS_PER_GROUP.bit_length() - 1)
    emask = (lane_i >= N_GROUPS) & (lane_i < N_GROUPS + N_EXPERTS) & (lane_grp == g_sel.astype(jnp.int32))
    le = jnp.where(emask, logits, neg)
    v1 = jnp.max(le, axis=-1, keepdims=True)
    i1 = jnp.min(jnp.where(emask & (le == v1), lane, big), axis=-1, keepdims=True)
    emask2 = emask & (lane != i1)
    le2 = jnp.where(emask2, logits, neg)
    v2 = jnp.max(le2, axis=-1, keepdims=True)
    i2 = jnp.min(jnp.where(emask2 & (le2 == v2), lane, big), axis=-1, keepdims=True)
    t = jnp.exp(v2 - v1)
    den = 1.0 + t
    gate1 = p_grp / den
    gate2 = p_grp * t / den
    rt_ref[...] = jnp.where(lane_i == 0, i1 - N_GROUPS,
                            jnp.where(lane_i == 1, i2 - N_GROUPS,
                                      jnp.where(lane_i == 2, gate1, jnp.where(lane_i == 3, gate2, 0.0))))


def _ln_router(mix, x, g, b, wr_hi, wr_lo, br, tm=256):
    m, d = x.shape
    tm = min(tm, m)
    rows = lambda w: pl.BlockSpec((tm, w), lambda i: (i, 0))
    full = lambda s: pl.BlockSpec(s, lambda i: (0, 0))
    return pl.pallas_call(
        _ln_router_kernel,
        grid=(m // tm,),
        in_specs=[rows(d), rows(d), full((1, d)), full((1, d)),
                  full((d, LANES)), full((d, LANES)), full((1, LANES))],
        out_specs=[rows(d), rows(d // 2), rows(LANES)],
        out_shape=[jax.ShapeDtypeStruct((m, d), F32), jax.ShapeDtypeStruct((m, d // 2), jnp.uint32),
                   jax.ShapeDtypeStruct((m, LANES), F32)],
        compiler_params=pltpu.CompilerParams(
            dimension_semantics=("parallel",), vmem_limit_bytes=VMEM_LIMIT),
        name="ln_router",
    )(mix, x, g.reshape(1, d), b.reshape(1, d), wr_hi, wr_lo, br)


def _moe_kernel(te_ref, nt_ref, ts_ref, tl_ref, order_ref, x_hbm, wg_ref, wu_ref, wd_ref,
                y_hbm, xbuf, ybuf, wgb, wub, wdb, sem_in, sem_out, *, tm):
    i = pl.program_id(0)
    n_tiles = nt_ref[0]
    slot = lax.rem(i, 2)
    n_tok = x_hbm.shape[0]
    n_pairs = y_hbm.shape[0] - tm

    def row_blocks(tile, block_fn):
        start, length = ts_ref[tile], tl_ref[tile]

        def rows(b):
            for r in range(b * MOE_DMA_BLOCK, (b + 1) * MOE_DMA_BLOCK):
                valid = r < length
                yield r, valid, order_ref[jnp.where(valid, start + r, 0)]

        for b in range(tm // MOE_DMA_BLOCK):
            @pl.when(b * MOE_DMA_BLOCK < length)
            def _():
                block_fn(b, rows(b))

    def block_rows(b):
        return pl.ds(b * MOE_DMA_BLOCK, MOE_DMA_BLOCK)

    def gather(tile, sl):
        def start_block(b, rows):
            for r, valid, p in rows:
                tok = jnp.where(valid, jnp.where(p >= n_tok, p - n_tok, p), 0)
                pltpu.make_async_copy(x_hbm.at[pl.ds(tok, 1)], xbuf.at[sl, pl.ds(r, 1)], sem_in.at[sl]).start()
        row_blocks(tile, start_block)

    def gather_wait(tile, sl):
        row_blocks(tile, lambda b, rows: pltpu.make_async_copy(
            x_hbm.at[block_rows(b)], xbuf.at[sl, block_rows(b)], sem_in.at[sl]).wait())

    def scatter_wait(tile):
        row_blocks(tile, lambda b, rows: pltpu.make_async_copy(
            ybuf.at[block_rows(b)], y_hbm.at[block_rows(b)], sem_out).wait())

    @pl.when(i == 0)
    def _():
        xbuf[...] = jnp.zeros_like(xbuf)
        gather(0, 0)

    @pl.when(i + 1 < n_tiles)
    def _():
        gather(i + 1, 1 - slot)

    @pl.when(i < n_tiles)
    def _():
        @pl.when((i == 0) | (te_ref[i] != te_ref[jnp.maximum(i - 1, 0)]))
        def _():
            wgb[...] = wg_ref[0, 0].astype(BF16)
            wub[...] = wu_ref[0, 0].astype(BF16)
            wdb[...] = wd_ref[0, 0].astype(BF16)

        gather_wait(i, slot)
        xb = _unpack_halves(xbuf[slot]).astype(BF16)
        hg = jnp.dot(xb, wgb[...], preferred_element_type=F32)
        hu = jnp.dot(xb, wub[...], preferred_element_type=F32)
        hact = hg * _sigmoid(hg) * hu
        y = jnp.dot(hact.astype(BF16), wdb[...], preferred_element_type=F32)

        @pl.when(i > 0)
        def _():
            scatter_wait(i - 1)

        ybuf[...] = _pack_halves(y)

        def scatter_block(b, rows):
            for r, valid, p in rows:
                dst = jnp.where(valid, p, n_pairs + r)
                pltpu.make_async_copy(ybuf.at[pl.ds(r, 1)], y_hbm.at[pl.ds(dst, 1)], sem_out).start()
        row_blocks(i, scatter_block)

        @pl.when(i == n_tiles - 1)
        def _():
            scatter_wait(i)
            fill = pltpu.make_async_copy(ybuf, y_hbm.at[pl.ds(y_hbm.shape[0] - tm, tm)], sem_out)
            fill.start()
            fill.wait()


def _moe(x1p, route, wg, wu, wd, layer, tm):
    m, dp = x1p.shape
    n_e, d, f = wg.shape[1], wg.shape[2], wg.shape[3]
    tm = min(tm, m)
    pairs = 2 * m
    nt_max = pairs // tm + n_e

    e_flat = jnp.concatenate([route[:, 0], route[:, 1]]).astype(jnp.int32)
    order = jnp.argsort(e_flat, stable=True).astype(jnp.int32)
    cnt = jnp.sum(e_flat[:, None] == jnp.arange(n_e, dtype=jnp.int32)[None, :], axis=0, dtype=jnp.int32)
    e_tiles = (cnt + tm - 1) // tm
    tile_end = jnp.cumsum(e_tiles)
    n_tiles = tile_end[-1:].astype(jnp.int32)
    tile_idx = jnp.arange(nt_max, dtype=jnp.int32)
    tile_e = jnp.minimum(jnp.searchsorted(tile_end, tile_idx, side='right'), n_e - 1).astype(jnp.int32)
    k_in_e = tile_idx - (tile_end - e_tiles)[tile_e]
    tile_start = ((jnp.cumsum(cnt) - cnt)[tile_e] + k_in_e * tm).astype(jnp.int32)
    tile_len = jnp.clip(cnt[tile_e] - k_in_e * tm, 0, tm).astype(jnp.int32)

    w_in_spec = pl.BlockSpec((1, 1, d, f), lambda i, te, nt, ts, tl, o: (layer, te[i], 0, 0))
    grid_spec = pltpu.PrefetchScalarGridSpec(
        num_scalar_prefetch=5,
        grid=(nt_max,),
        in_specs=[pl.BlockSpec(memory_space=pl.ANY), w_in_spec, w_in_spec,
                  pl.BlockSpec((1, 1, f, d), lambda i, te, nt, ts, tl, o: (layer, te[i], 0, 0))],
        out_specs=pl.BlockSpec(memory_space=pl.ANY),
        scratch_shapes=[pltpu.VMEM((2, tm, dp), jnp.uint32), pltpu.VMEM((tm, dp), jnp.uint32),
                        pltpu.VMEM((d, f), BF16), pltpu.VMEM((d, f), BF16), pltpu.VMEM((f, d), BF16),
                        pltpu.SemaphoreType.DMA((2,)), pltpu.SemaphoreType.DMA],
    )
    return pl.pallas_call(
        functools.partial(_moe_kernel, tm=tm),
        grid_spec=grid_spec,
        out_shape=jax.ShapeDtypeStruct((pairs + tm, dp), jnp.uint32),
        compiler_params=pltpu.CompilerParams(
            dimension_semantics=("arbitrary",), vmem_limit_bytes=MOE_VMEM_LIMIT),
        name="moe_experts",
    )(tile_e, n_tiles, tile_start, tile_len, order, x1p, wg, wu, wd)


def _ple_ln_kernel(x1_ref, y0_ref, y1_ref, rt_ref, p_ref, wp_ref, gd_ref, gu_ref, g_ref, b_ref,
                   x2_ref, x2b_ref):
    d = lambda a, w: jnp.dot(a, w, preferred_element_type=F32)
    x1 = x1_ref[...]
    proj = d(p_ref[...].astype(BF16), wp_ref[0])
    gate = _sigmoid(d(d(x1.astype(BF16), gd_ref[0]).astype(BF16), gu_ref[0]))
    rt = rt_ref[...]
    ffn = rt[:, 2:3] * _unpack_halves(y0_ref[...]) + rt[:, 3:4] * _unpack_halves(y1_ref[...])
    y = DN_ALPHA * x1 + ffn + proj * gate
    x2 = _layer_norm(y, g_ref[...], b_ref[...])
    x2_ref[...] = x2
    x2b_ref[...] = x2.astype(BF16)


def _ple_ln(x1, y, route, p, wp, gd, gu, g, b, layer, tm=128):
    m, d = x1.shape
    pd = p.shape[1]
    tm = min(tm, m)
    nb = m // tm
    rows = lambda w: pl.BlockSpec((tm, w), lambda i: (i, 0))
    full = lambda s: pl.BlockSpec(s, lambda i: (0, 0))
    stack = lambda r, c: pl.BlockSpec((1, r, c), lambda i: (layer, 0, 0))
    return pl.pallas_call(
        _ple_ln_kernel,
        grid=(nb,),
        in_specs=[rows(d), rows(d // 2), pl.BlockSpec((tm, d // 2), lambda i: (i + nb, 0)), rows(LANES),
                  rows(pd), stack(pd, d), stack(d, pd), stack(pd, d), full((1, d)), full((1, d))],
        out_specs=[rows(d), rows(d)],
        out_shape=[jax.ShapeDtypeStruct((m, d), F32), jax.ShapeDtypeStruct((m, d), BF16)],
        compiler_params=pltpu.CompilerParams(
            dimension_semantics=("parallel",), vmem_limit_bytes=VMEM_LIMIT),
        name="ple_ln",
    )(x1, y, y, route, p, wp, gd, gu, g.reshape(1, d), b.reshape(1, d))


def _pad_rows(w, rows, offset):
    return jnp.pad(w, ((0, 0), (offset, rows - offset - w.shape[1]), (0, 0))).astype(BF16)


def _layer(layer, x, xb, p, wt, conv_w, a_log, dt_bias, norm_w, mu, w0, w2p, a0, a2p, g2p,
           k_k, k_a, r_k, lnx_g, lnx_b, w_out, ln1_g, ln1_b, w_grp, b_grp, w_rt, b_rt, wg, wu, wd,
           wp, gd, gu, ln2_g, ln2_b, *, batch, tb_gdn=1024, tb_rwkv=1024):
    m, d_model = x.shape
    t = m // batch
    d_mix = w_out.shape[1]
    d_delta = d_mix // 2
    d_rwkv = d_mix - d_delta

    n_ab = 2 * (d_delta // GDN_HEAD_DIM)
    n_gdn, n_rwkv = 4 * d_delta + AB_PAD, wt.shape[1] - 4 * d_delta - n_ab
    h_gdn = _matmul_nt(xb, wt, layer, 0, n_gdn, 1024, _pick_tile(n_gdn, (768, 512, 256)))
    h_rwkv = _matmul_nt(xb, wt, layer, 4 * d_delta + n_ab, n_rwkv, 512,
                        _pick_tile(n_rwkv, (1280, 896, 512, 256)))
    o_gdn = _gdn(h_gdn.reshape(batch, t, -1), conv_w, a_log, dt_bias, norm_w, d_delta, tb_gdn)
    o_rwkv = _rwkv(h_rwkv.reshape(batch, t, -1), 0, d_rwkv, mu, w0, w2p, a0, a2p, g2p, k_k, k_a, r_k,
                   lnx_g, lnx_b, layer, tb_rwkv)
    mix = _matmul(o_gdn.reshape(m, d_delta), o_rwkv.reshape(m, d_rwkv), w_out, layer)

    w_router = jnp.concatenate(
        [w_grp, w_rt, jnp.zeros((d_model, LANES - N_GROUPS - N_EXPERTS), F32)], axis=1).astype(F32)
    wr_hi, wr_lo = _split_bf16(w_router)
    b_router = jnp.concatenate(
        [b_grp, b_rt, jnp.zeros((LANES - N_GROUPS - N_EXPERTS,), F32)]).astype(F32).reshape(1, LANES)
    x1, x1p, route = _ln_router(mix, x, ln1_g, ln1_b, wr_hi, wr_lo, b_router)

    y = _moe(x1p, route, wg, wu, wd, layer, MOE_TILE)
    return _ple_ln(x1, y, route, p, wp, gd, gu, ln2_g, ln2_b, layer)


def kernel(x, p, w_in, gdn_conv_w, gdn_a_log, gdn_dt_bias, gdn_norm_w, rwkv_mu, rwkv_w0, rwkv_w2, rwkv_a0, rwkv_a2, rwkv_g2, rwkv_k_k, rwkv_k_a, rwkv_r_k, rwkv_lnx_g, rwkv_lnx_b, w_out, ln1_g, ln1_b, moe_w_grp, moe_b_grp, moe_w_rt, moe_b_rt, moe_w_gate, moe_w_up, moe_w_down, ple_w_proj, ple_w_gate_down, ple_w_gate_up, ln2_g, ln2_b):
    batch, t, d_model = x.shape
    m = batch * t
    xf = x.reshape(m, d_model).astype(F32)
    xb = xf.astype(BF16)

    n_layers = w_in.shape[0]
    wt = jnp.swapaxes(w_in, 1, 2).astype(BF16)
    w_out_b = w_out.astype(BF16)
    wp_b, gd_b, gu_b = (w.astype(BF16) for w in (ple_w_proj, ple_w_gate_down, ple_w_gate_up))
    lora_w, lora_a = rwkv_w2.shape[1], rwkv_a2.shape[1]
    w2p = _pad_rows(rwkv_w2, LORA_PAD, 0)
    a2p = _pad_rows(rwkv_a2, LORA_PAD, lora_w)
    g2p = _pad_rows(rwkv_g2, LORA_PAD, lora_w + lora_a)

    for i in range(n_layers):
        xf, xb = _layer(
            i, xf, xb, p[i].reshape(m, -1), wt, gdn_conv_w[i], gdn_a_log[i], gdn_dt_bias[i],
            gdn_norm_w[i], rwkv_mu[i], rwkv_w0[i], w2p, rwkv_a0[i], a2p, g2p,
            rwkv_k_k[i], rwkv_k_a[i], rwkv_r_k[i], rwkv_lnx_g[i], rwkv_lnx_b[i], w_out_b,
            ln1_g[i], ln1_b[i], moe_w_grp[i], moe_b_grp[i], moe_w_rt[i], moe_b_rt[i],
            moe_w_gate, moe_w_up, moe_w_down, wp_b, gd_b, gu_b, ln2_g[i], ln2_b[i], batch=batch)
    return xf.reshape(batch, t, d_model)
```
